```python
import math
import jax, jax.numpy as jnp
from jax import lax
import numpy as np

D_MODEL = 1024
BATCH = 2
SEQ = 8192
DEPTH = 2

CHUNK = 64
HG_HEADS = 4
HG_DK = 128
HG_DV = 128
HG_KWIDTH = HG_HEADS * HG_DK
HG_VWIDTH = HG_HEADS * HG_DV
GDN_HEADS = 4
GDN_DK = 128
GDN_DV = 128
GDN_QK = GDN_HEADS * GDN_DK
GDN_V = GDN_HEADS * GDN_DV
CONV_K = 4
PEER_HEADS = 8
PEER_DKEY = 256
PEER_NKEYS = 128
PEER_TOPK = 16
PEER_EXPERTS = PEER_NKEYS * PEER_NKEYS
PEER_BLOCK = 128
EPS = 1e-6
IN_SIZES = (HG_KWIDTH, HG_KWIDTH, HG_VWIDTH, HG_VWIDTH, 2 * GDN_QK + GDN_V, GDN_V, GDN_HEADS, GDN_HEADS, D_MODEL, D_MODEL)
IN_COLS = sum(IN_SIZES)

kernel_name = "hybrid_hgrn2_gdn_peer_adaln"


def _offsets(sizes):
    offs, acc = [], 0
    for s in sizes[:-1]:
        acc += s
        offs.append(acc)
    return offs


def rmsnorm(x, g):
    xf = x.astype(jnp.float32)
    y = xf * lax.rsqrt(jnp.mean(xf * xf, axis=-1, keepdims=True) + EPS)
    return (y * g.astype(jnp.float32)).astype(x.dtype)


def l2norm(x):
    return x * lax.rsqrt(jnp.sum(x * x, axis=-1, keepdims=True) + EPS)


def to_chunks(t):
    B, S, H, d = t.shape
    return t.reshape(B, S // CHUNK, CHUNK, H, d).transpose(1, 0, 3, 2, 4)


def from_chunks(t):
    nC, B, H, C, d = t.shape
    return t.transpose(1, 0, 3, 2, 4).reshape(B, nC * C, H, d)


def causal_conv(x, w):
    return lax.conv_general_dilated(x, w[:, None, :], window_strides=(1,), padding=[(CONV_K - 1, 0)],
                                    dimension_numbers=('NWC', 'WIO', 'NWC'), feature_group_count=x.shape[-1])


def hgrn2(q, f, i, g, lb, norm_g):
    B, S, _ = q.shape
    fgate = lb + (1.0 - lb) * jax.nn.sigmoid(f)
    logf = jnp.log(fgate)
    k = 1.0 - fgate
    q = jax.nn.silu(q)
    qc = to_chunks(q.reshape(B, S, HG_HEADS, HG_DK))
    kc = to_chunks(k.reshape(B, S, HG_HEADS, HG_DK))
    lc = to_chunks(logf.reshape(B, S, HG_HEADS, HG_DK))
    vc = to_chunks(i.reshape(B, S, HG_HEADS, HG_DV))
    causal = jnp.tril(jnp.ones((CHUNK, CHUNK), dtype=bool))[:, :, None]

    def step(state, inp):
        qq, kk, ll, vv = inp
        b = jnp.cumsum(ll, axis=2)
        o = jnp.einsum('bhtk,bhkv->bhtv', qq * jnp.exp(b), state)
        diff = jnp.where(causal, b[:, :, :, None, :] - b[:, :, None, :, :], -jnp.inf)
        att = jnp.sum(qq[:, :, :, None, :] * kk[:, :, None, :, :] * jnp.exp(diff), axis=-1)
        o = o + jnp.einsum('bhts,bhsv->bhtv', att, vv)
        b_last = b[:, :, -1:, :]
        state = jnp.exp(b_last[:, :, 0, :])[..., None] * state + jnp.einsum('bhsk,bhsv->bhkv', kk * jnp.exp(b_last - b), vv)
        return state, o

    state0 = jnp.zeros((B, HG_HEADS, HG_DK, HG_DV), jnp.float32)
    _, o = lax.scan(step, state0, (qc, kc, lc, vc))
    o = from_chunks(o)
    o = rmsnorm(o, norm_g) * jax.nn.silu(g.reshape(B, S, HG_HEADS, HG_DV))
    return o.reshape(B, S, HG_VWIDTH)


def gated_deltanet(qkv, g, beta_raw, a_raw, conv_w, a_log, dt_bias, norm_g):
    B, S, _ = qkv.shape
    qkv = jax.nn.silu(causal_conv(qkv, conv_w))
    q, k, v = jnp.split(qkv, [GDN_QK, 2 * GDN_QK], axis=-1)
    q = l2norm(q.reshape(B, S, GDN_HEADS, GDN_DK)) * (GDN_DK ** -0.5)
    k = l2norm(k.reshape(B, S, GDN_HEADS, GDN_DK))
    v = v.reshape(B, S, GDN_HEADS, GDN_DV)
    beta = jax.nn.sigmoid(beta_raw)
    logdec = -jnp.exp(a_log) * jax.nn.softplus(a_raw + dt_bias)
    qc, kc, vc = to_chunks(q), to_chunks(k), to_chunks(v)
    bc = to_chunks(beta[..., None])[..., 0]
    b = jnp.cumsum(to_chunks(logdec[..., None])[..., 0], axis=-1)
    causal = jnp.tril(jnp.ones((CHUNK, CHUNK), dtype=bool))
    strict = jnp.tril(jnp.ones((CHUNK, CHUNK), dtype=bool), k=-1)
    dec = jnp.exp(jnp.where(causal, b[..., :, None] - b[..., None, :], -jnp.inf))
    kb = kc * bc[..., None]
    a_mat = jnp.where(strict, jnp.einsum('nbhid,nbhjd->nbhij', kb, kc) * dec, 0.0)
    u = lax.linalg.triangular_solve(a_mat, vc * bc[..., None], left_side=True, lower=True, unit_diagonal=True)
    w = lax.linalg.triangular_solve(a_mat, kb * jnp.exp(b)[..., None], left_side=True, lower=True, unit_diagonal=True)
    aqk = jnp.einsum('nbhid,nbhjd->nbhij', qc, kc) * dec
    qe = qc * jnp.exp(b)[..., None]
    kd = kc * jnp.exp(b[..., -1:] - b)[..., None]
    b_last = jnp.exp(b[..., -1])

    def step(state, inp):
        uu, ww, aa, qq, kk, bl = inp
        v_new = uu - jnp.einsum('bhck,bhkv->bhcv', ww, state)
        o = jnp.einsum('bhck,bhkv->bhcv', qq, state) + jnp.einsum('bhcs,bhsv->bhcv', aa, v_new)
        state = bl[..., None, None] * state + jnp.einsum('bhsk,bhsv->bhkv', kk, v_new)
        return state, o

    state0 = jnp.zeros((B, GDN_HEADS, GDN_DK, GDN_DV), jnp.float32)
    _, o = lax.scan(step, state0, (u, w, aqk, qe, kd, b_last))
    o = from_chunks(o)
    o = rmsnorm(o, norm_g) * jax.nn.silu(g.reshape(B, S, GDN_HEADS, GDN_DV))
    return o.reshape(B, S, GDN_V)


def token_mixer(h, lb, w_in, hg_norm_g, conv_w, a_log, dt_bias, gdn_norm_g, w_br_hg, w_br_gdn, w_out):
    dt = h.dtype
    proj = (h @ w_in).astype(jnp.float32)
    hq, hf, hi, hg, qkv, gg, beta_raw, a_raw, gate_hg, gate_gdn = jnp.split(proj, _offsets(IN_SIZES), axis=-1)
    y_hg = hgrn2(hq, hf, hi, hg, lb, hg_norm_g)
    y_gdn = gated_deltanet(qkv, gg, beta_raw, a_raw, conv_w.astype(jnp.float32),
                           a_log.astype(jnp.float32), dt_bias.astype(jnp.float32), gdn_norm_g)
    merged = (jax.nn.sigmoid(gate_hg).astype(dt) * (y_hg.astype(dt) @ w_br_hg)
              + jax.nn.sigmoid(gate_gdn).astype(dt) * (y_gdn.astype(dt) @ w_br_gdn))
    return merged @ w_out


def peer(h, wq, subkeys, u_tab, v_tab):
    B, S, D = h.shape
    hb = h.reshape(B * S // PEER_BLOCK, PEER_BLOCK, D)

    def block(xb):
        P = xb.shape[0]
        q = (xb @ wq).reshape(P, PEER_HEADS, 2, PEER_DKEY // 2)
        sc = jnp.einsum('phzd,hznd->phzn', q, subkeys)
        sv, si = lax.top_k(sc, PEER_TOPK)
        cand = (sv[:, :, 0, :, None] + sv[:, :, 1, None, :]).reshape(P, PEER_HEADS, PEER_TOPK * PEER_TOPK)
        cidx = (si[:, :, 0, :, None] * PEER_NKEYS + si[:, :, 1, None, :]).reshape(P, PEER_HEADS, PEER_TOPK * PEER_TOPK)
        top_s, top_c = lax.top_k(cand, PEER_TOPK)
        experts = jnp.take_along_axis(cidx, top_c, axis=-1)
        gate = jax.nn.softmax(top_s.astype(jnp.float32), axis=-1).astype(xb.dtype)
        ue = jnp.take(u_tab, experts, axis=0)
        ve = jnp.take(v_tab, experts, axis=0)
        act = jax.nn.gelu(jnp.einsum('pd,phkd->phk', xb, ue), approximate=False) * gate
        return jnp.einsum('phk,phkd->pd', act, ve)

    return lax.map(block, hb).reshape(B, S, D)


def setup_inputs(seed: int = 0) -> dict:
    key = jax.random.key(seed)
    ks = jax.random.split(key, 24)
    f32 = jnp.float32
    nrm = lambda k, shp, s: (jax.random.normal(k, shp, f32) * s)
    dt = jnp.exp(jax.random.uniform(ks[10], (DEPTH, GDN_HEADS), f32, minval=math.log(1e-3), maxval=math.log(1e-1)))
    return {
        'x': nrm(ks[0], (BATCH, SEQ, D_MODEL), 1.0),
        'c': nrm(ks[1], (BATCH, D_MODEL), 1.0),
        'ada_w': nrm(ks[2], (DEPTH, D_MODEL, 6 * D_MODEL), 0.5 * D_MODEL ** -0.5),
        'ada_b': nrm(ks[3], (DEPTH, 6 * D_MODEL), 0.01),
        'norm1_g': 1.0 + nrm(ks[4], (DEPTH, D_MODEL), 0.02),
        'norm2_g': 1.0 + nrm(ks[5], (DEPTH, D_MODEL), 0.02),
        'final_g': 1.0 + nrm(ks[6], (D_MODEL,), 0.02),
        'w_in': nrm(ks[7], (DEPTH, D_MODEL, IN_COLS), D_MODEL ** -0.5),
        'hg_lb_logits': nrm(ks[8], (DEPTH, HG_KWIDTH), 0.5),
        'hg_norm_g': 1.0 + nrm(ks[9], (DEPTH, HG_DV), 0.02),
        'gdn_conv_w': nrm(ks[11], (DEPTH, CONV_K, 2 * GDN_QK + GDN_V), CONV_K ** -0.5),
        'gdn_a_log': jnp.log(jax.random.uniform(ks[12], (DEPTH, GDN_HEADS), f32, minval=1.0, maxval=16.0)),
        'gdn_dt_bias': dt + jnp.log(-jnp.expm1(-dt)),
        'gdn_norm_g': 1.0 + nrm(ks[13], (DEPTH, GDN_DV), 0.02),
        'w_branch_hg': nrm(ks[14], (DEPTH, HG_VWIDTH, D_MODEL), HG_VWIDTH ** -0.5),
        'w_branch_gdn': nrm(ks[15], (DEPTH, GDN_V, D_MODEL), GDN_V ** -0.5),
        'w_out': nrm(ks[16], (DEPTH, D_MODEL, D_MODEL), D_MODEL ** -0.5),
        'peer_wq': nrm(ks[17], (DEPTH, D_MODEL, PEER_HEADS * PEER_DKEY), D_MODEL ** -0.5),
        'peer_subkeys': nrm(ks[18], (DEPTH, PEER_HEADS, 2, PEER_NKEYS, PEER_DKEY // 2), (PEER_DKEY // 2) ** -0.5),
        'peer_u': nrm(ks[19], (DEPTH, PEER_EXPERTS, D_MODEL), D_MODEL ** -0.5),
        'peer_v': nrm(ks[20], (DEPTH, PEER_EXPERTS, D_MODEL), PEER_HEADS ** -0.5),
    }


def reference(x, c, ada_w, ada_b, norm1_g, norm2_g, final_g, w_in, hg_lb_logits, hg_norm_g,
              gdn_conv_w, gdn_a_log, gdn_dt_bias, gdn_norm_g, w_branch_hg, w_branch_gdn, w_out,
              peer_wq, peer_subkeys, peer_u, peer_v):
    sm = jax.nn.softmax(hg_lb_logits.astype(jnp.float32), axis=0)
    lower_bounds = jnp.cumsum(sm, axis=0) - sm[0:1]
    cond = jax.nn.silu(c)
    for l in range(DEPTH):
        mod = (cond @ ada_w[l] + ada_b[l])[:, None, :]
        sh1, sc1, gt1, sh2, sc2, gt2 = jnp.split(mod, 6, axis=-1)
        h = rmsnorm(x, norm1_g[l]) * (1.0 + sc1) + sh1
        y = token_mixer(h, lower_bounds[l], w_in[l], hg_norm_g[l], gdn_conv_w[l], gdn_a_log[l], gdn_dt_bias[l],
                        gdn_norm_g[l], w_branch_hg[l], w_branch_gdn[l], w_out[l])
        x = x + gt1 * y.astype(x.dtype)
        h = rmsnorm(x, norm2_g[l]) * (1.0 + sc2) + sh2
        x = x + gt2 * peer(h, peer_wq[l], peer_subkeys[l], peer_u[l], peer_v[l]).astype(x.dtype)
    return rmsnorm(x, final_g)
```

```python
import functools
import math

import jax
import jax.numpy as jnp
from jax import lax
from jax.experimental import pallas as pl
from jax.experimental.pallas import tpu as pltpu

F32 = jnp.float32
BF16 = jnp.bfloat16

D_MODEL = 1024
CHUNK = 64
N_HEADS = 4
HEAD_DIM = 128
BRANCH_W = N_HEADS * HEAD_DIM
CONV_K = 4
PEER_HEADS = 8
PEER_NKEYS = 128
PEER_HALF = 128
PEER_TOPK = 16
EPS = 1e-6
NEG = -1e30
EXP_CLAMP = 60.0

LANES = 128
SUBLANES = 8
VMEM_LIMIT = 56 * 1024 * 1024

MAIN_COLS = 8 * BRANCH_W + 2 * D_MODEL


def _cparams(n_axes):
    return pltpu.CompilerParams(
        dimension_semantics=("arbitrary",) * n_axes, vmem_limit_bytes=VMEM_LIMIT)


def _sigmoid(x):
    return 1.0 / (1.0 + jnp.exp(-x))


def _silu(x):
    return x * _sigmoid(x)


def _dot(a, b):
    return jnp.dot(a.astype(BF16), b.astype(BF16), preferred_element_type=F32)


def _dot_nt(a, b):
    return lax.dot_general(a.astype(BF16), b.astype(BF16), (((1,), (1,)), ((), ())),
                           preferred_element_type=F32)


def _dot_tn(a, b):
    return lax.dot_general(a.astype(BF16), b.astype(BF16), (((0,), (0,)), ((), ())),
                           preferred_element_type=F32)


def _mm(a, b):
    return _dot(a, b)


def _dot_split3(m_bf16, x):
    h1 = x.astype(BF16)
    r1 = x - h1.astype(F32)
    h2 = r1.astype(BF16)
    h3 = (r1 - h2.astype(F32)).astype(BF16)
    d = lambda h: jnp.dot(m_bf16, h, preferred_element_type=F32)
    return d(h1) + d(h2) + d(h3)


def _rms(x, g):
    return x * lax.rsqrt(jnp.mean(x * x, axis=-1, keepdims=True) + EPS) * g


def _adaln_kernel(c_ref, w_ref, b_ref, o_ref):
    cond = _silu(c_ref[...])
    o_ref[0] = jnp.dot(cond, w_ref[0], precision=lax.Precision.HIGHEST,
                       preferred_element_type=F32) + b_ref[0]


def _adaln(c_pad, ada_w, ada_b):
    depth, d, cols = ada_w.shape
    tn = 1536
    return pl.pallas_call(
        _adaln_kernel,
        grid=(depth, cols // tn),
        in_specs=[
            pl.BlockSpec((SUBLANES, d), lambda l, j: (0, 0)),
            pl.BlockSpec((1, d, tn), lambda l, j: (l, 0, j)),
            pl.BlockSpec((1, 1, tn), lambda l, j: (l, 0, j)),
        ],
        out_specs=pl.BlockSpec((1, SUBLANES, tn), lambda l, j: (l, 0, j)),
        out_shape=jax.ShapeDtypeStruct((depth, SUBLANES, cols), F32),
        compiler_params=_cparams(2),
        name="adaln",
    )(c_pad, ada_w, ada_b.reshape(depth, 1, cols))


def _proj_kernel(x_ref, g_ref, sc_ref, sh_ref, w_ref, ws_ref, o_ref, os_ref, h_ref):
    @pl.when(pl.program_id(1) == 0)
    def _():
        h = _rms(x_ref[...], g_ref[...]) * (1.0 + sc_ref[0]) + sh_ref[0]
        hb = h.astype(BF16)
        h_ref[...] = hb
        os_ref[...] = jnp.dot(hb, ws_ref[...], preferred_element_type=F32)

    o_ref[...] = jnp.dot(h_ref[...], w_ref[...], preferred_element_type=F32)


def _proj(x, g, sc, sh, w_main, w_small, seq):
    n, d = x.shape
    cols = w_main.shape[1]
    tm, tn = 1024, 1536
    tpb = seq // tm
    return pl.pallas_call(
        _proj_kernel,
        grid=(n // tm, cols // tn),
        in_specs=[
            pl.BlockSpec((tm, d), lambda i, j: (i, 0)),
            pl.BlockSpec((1, d), lambda i, j: (0, 0)),
            pl.BlockSpec((1, 1, d), lambda i, j: (i // tpb, 0, 0)),
            pl.BlockSpec((1, 1, d), lambda i, j: (i // tpb, 0, 0)),
            pl.BlockSpec((d, tn), lambda i, j: (0, j)),
            pl.BlockSpec((d, LANES), lambda i, j: (0, 0)),
        ],
        out_specs=[
            pl.BlockSpec((tm, tn), lambda i, j: (i, j)),
            pl.BlockSpec((tm, LANES), lambda i, j: (i, 0)),
        ],
        out_shape=[
            jax.ShapeDtypeStruct((n, cols), F32),
            jax.ShapeDtypeStruct((n, LANES), F32),
        ],
        scratch_shapes=[pltpu.VMEM((tm, d), BF16)],
        compiler_params=_cparams(2),
        name="in_proj",
    )(x, g, sc, sh, w_main, w_small)


def _hgrn2_kernel(q_ref, f_ref, i_ref, g_ref, lb_ref, ng_ref, tri_ref, y_ref, st_ref, *, n_chunks):
    @pl.when(pl.program_id(1) == 0)
    def _():
        st_ref[...] = jnp.zeros_like(st_ref)

    lb = lb_ref[...]
    ng = ng_ref[...]
    tri = tri_ref[...]
    sub = CHUNK // 4
    row = lax.broadcasted_iota(jnp.int32, (CHUNK, CHUNK), 0)
    col = lax.broadcasted_iota(jnp.int32, (CHUNK, CHUNK), 1)
    causal = col <= row

    def chunk_body(c, carry):
        rows = pl.ds(pl.multiple_of(c * CHUNK, CHUNK), CHUNK)
        fgate = lb + (1.0 - lb) * _sigmoid(f_ref[rows, :])
        logf = jnp.log(fgate)
        b_all = _dot_split3(tri, logf)
        for h in range(N_HEADS):
            hs = slice(h * HEAD_DIM, (h + 1) * HEAD_DIM)
            b = b_all[:, hs]
            kk = 1.0 - fgate[:, hs]
            qq = _silu(q_ref[rows, hs])
            vv = i_ref[rows, hs]
            st = st_ref[h]
            o = _dot_nt(qq * jnp.exp(b), st)
            att_rows = []
            for blk in range(4):
                r0 = blk * sub
                ref_b = b[r0:r0 + 1, :]
                qe = qq[r0:r0 + sub, :] * jnp.exp(b[r0:r0 + sub, :] - ref_b)
                ke = kk * jnp.exp(jnp.minimum(ref_b - b, EXP_CLAMP))
                att_rows.append(_dot_nt(qe, ke))
            att = jnp.where(causal, jnp.concatenate(att_rows, axis=0), 0.0)
            o = o + _dot(att, vv)
            b_last = b[CHUNK - 1:CHUNK, :]
            kd = kk * jnp.exp(b_last - b)
            st_ref[h] = st * jnp.exp(b_last) + _dot_tn(vv, kd)
            y = _rms(o, ng) * _silu(g_ref[rows, hs])
            y_ref[rows, hs] = y.astype(y_ref.dtype)
        return carry

    lax.fori_loop(0, n_chunks, chunk_body, 0)


def _hgrn2(proj, lb, norm_g, tri, batch, seq):
    t = 256
    spb = seq // t
    col = lambda k: (lambda b, s: (b * spb + s, k))
    const = lambda b, s: (0, 0)
    return pl.pallas_call(
        functools.partial(_hgrn2_kernel, n_chunks=t // CHUNK),
        grid=(batch, spb),
        in_specs=[
            pl.BlockSpec((t, BRANCH_W), col(0)),
            pl.BlockSpec((t, BRANCH_W), col(1)),
            pl.BlockSpec((t, BRANCH_W), col(2)),
            pl.BlockSpec((t, BRANCH_W), col(3)),
            pl.BlockSpec((1, BRANCH_W), const),
            pl.BlockSpec((1, HEAD_DIM), const),
            pl.BlockSpec((CHUNK, CHUNK), const),
        ],
        out_specs=pl.BlockSpec((t, BRANCH_W), lambda b, s: (b * spb + s, 0)),
        out_shape=jax.ShapeDtypeStruct((batch * seq, BRANCH_W), BF16),
        scratch_shapes=[pltpu.VMEM((N_HEADS, HEAD_DIM, HEAD_DIM), F32)],
        compiler_params=_cparams(2),
        name="hgrn2",
    )(proj, proj, proj, proj, lb, norm_g, tri)


def _softplus(x):
    return jnp.maximum(x, 0.0) + jnp.log(1.0 + jnp.exp(-jnp.abs(x)))


def _gdn_kernel(q_ref, k_ref, v_ref, g_ref, ba_ref, cw_ref, alog_ref, dtb_ref, ng_ref, tri_ref,
                y_ref, xb_ref, qn_ref, kn_ref, vn_ref, st_ref, *, t, n_chunks):
    first = pl.program_id(1) == 0

    @pl.when(first)
    def _():
        st_ref[...] = jnp.zeros_like(st_ref)
        xb_ref[:, 0:SUBLANES, :] = jnp.zeros((3, SUBLANES, BRANCH_W), F32)

    @pl.when(jnp.logical_not(first))
    def _():
        xb_ref[:, 0:SUBLANES, :] = xb_ref[:, t:t + SUBLANES, :]

    for j, (src, dst, scale) in enumerate(((q_ref, qn_ref, HEAD_DIM ** -0.5), (k_ref, kn_ref, 1.0),
                                           (v_ref, vn_ref, None))):
        xb_ref[j, SUBLANES:SUBLANES + t, :] = src[...]
        acc = jnp.zeros((t, BRANCH_W), F32)
        for tap in range(CONV_K):
            off = SUBLANES - (CONV_K - 1) + tap
            w = cw_ref[tap:tap + 1, j * BRANCH_W:(j + 1) * BRANCH_W]
            acc = acc + xb_ref[j, off:off + t, :] * w
        acc = _silu(acc)
        if scale is None:
            dst[...] = acc
        else:
            for h in range(N_HEADS):
                hs = slice(h * HEAD_DIM, (h + 1) * HEAD_DIM)
                xh = acc[:, hs]
                dst[:, hs] = xh * (lax.rsqrt(jnp.sum(xh * xh, axis=-1, keepdims=True) + EPS) * scale)

    tri = tri_ref[...]
    ng = ng_ref[...]
    row = lax.broadcasted_iota(jnp.int32, (CHUNK, CHUNK), 0)
    col = lax.broadcasted_iota(jnp.int32, (CHUNK, CHUNK), 1)
    causal = col <= row
    strict = col < row
    eye = (col == row).astype(F32)
    blk16 = (row // 16) == (col // 16)
    blk32 = (row // 32) == (col // 32)

    def chunk_body(c, carry):
        rows = pl.ds(pl.multiple_of(c * CHUNK, CHUNK), CHUNK)
        ba = ba_ref[rows, :]
        beta_all = _sigmoid(ba)
        logdec = -jnp.exp(alog_ref[...]) * _softplus(ba + dtb_ref[...])
        b_all = _dot_split3(tri, logdec)
        b_pad = jnp.concatenate([b_all, jnp.zeros((LANES - CHUNK, LANES), F32)], axis=0)
        b_t = b_pad.T
        for h in range(N_HEADS):
            hs = slice(h * HEAD_DIM, (h + 1) * HEAD_DIM)
            lane = N_HEADS + h
            bcol = b_all[:, lane:lane + 1]
            brow = b_t[lane:lane + 1, 0:CHUNK]
            beta = beta_all[:, h:h + 1]
            dec = jnp.exp(jnp.where(causal, bcol - brow, NEG))
            qh = qn_ref[rows, hs]
            kh = kn_ref[rows, hs]
            vh = vn_ref[rows, hs]
            kb = kh * beta
            a_mat = jnp.where(strict, _dot_nt(kb, kh) * dec, 0.0)
            a_diag = jnp.where(blk16, a_mat, 0.0)
            p = eye - a_diag
            x = a_diag
            for _ in range(3):
                x = _mm(x, x)
                p = p + _mm(p, x)
            p = p - _mm(_mm(p, jnp.where(blk32, a_mat, 0.0) - a_diag), p)
            p = p - _mm(_mm(p, jnp.where(blk32, 0.0, a_mat)), p)
            p_low = jnp.where(strict, p, 0.0)
            rhs = jnp.concatenate([vh * beta, kb * jnp.exp(bcol)], axis=1)
            uw = rhs + _mm(p_low, rhs)
            u = uw[:, :HEAD_DIM]
            w = uw[:, HEAD_DIM:]
            aqk = _dot_nt(qh, kh) * dec
            qe = qh * jnp.exp(bcol)
            b_last = bcol[CHUNK - 1:CHUNK, :]
            kd = kh * jnp.exp(b_last - bcol)
            st = st_ref[h]
            ws = _dot(jnp.concatenate([w, qe], axis=0), st)
            v_new = u - ws[:CHUNK]
            o = ws[CHUNK:] + _dot(aqk, v_new)
            st_ref[h] = st * jnp.exp(b_last) + _dot_tn(kd, v_new)
            y = _rms(o, ng) * _silu(g_ref[rows, hs])
            y_ref[rows, hs] = y.astype(y_ref.dtype)
        return carry

    lax.fori_loop(0, n_chunks, chunk_body, 0)


def _gdn(proj, ba, conv_w, alog_row, dtb_row, norm_g, tri, batch, seq):
    t = 256
    spb = seq // t
    col = lambda k: (lambda b, s: (b * spb + s, k))
    const = lambda b, s: (0, 0)
    return pl.pallas_call(
        functools.partial(_gdn_kernel, t=t, n_chunks=t // CHUNK),
        grid=(batch, spb),
        in_specs=[
            pl.BlockSpec((t, BRANCH_W), col(4)),
            pl.BlockSpec((t, BRANCH_W), col(5)),
            pl.BlockSpec((t, BRANCH_W), col(6)),
            pl.BlockSpec((t, BRANCH_W), col(7)),
            pl.BlockSpec((t, LANES), col(0)),
            pl.BlockSpec((CONV_K, 3 * BRANCH_W), const),
            pl.BlockSpec((1, LANES), const),
            pl.BlockSpec((1, LANES), const),
            pl.BlockSpec((1, HEAD_DIM), const),
            pl.BlockSpec((CHUNK, CHUNK), const),
        ],
        out_specs=pl.BlockSpec((t, BRANCH_W), lambda b, s: (b * spb + s, 0)),
        out_shape=jax.ShapeDtypeStruct((batch * seq, BRANCH_W), BF16),
        scratch_shapes=[
            pltpu.VMEM((3, t + 2 * SUBLANES, BRANCH_W), F32),
            pltpu.VMEM((t, BRANCH_W), F32),
            pltpu.VMEM((t, BRANCH_W), F32),
            pltpu.VMEM((t, BRANCH_W), F32),
            pltpu.VMEM((N_HEADS, HEAD_DIM, HEAD_DIM), F32),
        ],
        compiler_params=_cparams(2),
        name="gdn",
    )(proj, proj, proj, proj, ba, conv_w, alog_row, dtb_row, norm_g, tri)


def _merge_kernel(yh_ref, yg_ref, gh_ref, gg_ref, x_ref, gt_ref, wbh_ref, wbg_ref, wo_ref,
                  g2_ref, sc_ref, sh_ref, wq_ref, xo_ref, h2_ref, q_ref):
    merged = (_sigmoid(gh_ref[...]) * jnp.dot(yh_ref[...], wbh_ref[...], preferred_element_type=F32)
              + _sigmoid(gg_ref[...]) * jnp.dot(yg_ref[...], wbg_ref[...], preferred_element_type=F32))
    y = jnp.dot(merged.astype(BF16), wo_ref[...], preferred_element_type=F32)
    xn = x_ref[...] + gt_ref[0] * y
    xo_ref[...] = xn
    h2 = (_rms(xn, g2_ref[...]) * (1.0 + sc_ref[0]) + sh_ref[0]).astype(BF16)
    h2_ref[...] = h2
    q_ref[...] = jnp.dot(h2, wq_ref[...], preferred_element_type=F32)


def _merge(y_hg, y_gdn, proj, x, gt1, w_bh, w_bg, w_out, g2, sc2, sh2, wq, seq):
    n, d = x.shape
    tm = 512
    tpb = seq // tm
    qw = wq.shape[1]
    const = lambda i: (0, 0)
    per_b = lambda i: (i // tpb, 0, 0)
    gate_blk = 8 * BRANCH_W // D_MODEL
    return pl.pallas_call(
        _merge_kernel,
        grid=(n // tm,),
        in_specs=[
            pl.BlockSpec((tm, BRANCH_W), lambda i: (i, 0)),
            pl.BlockSpec((tm, BRANCH_W), lambda i: (i, 0)),
            pl.BlockSpec((tm, d), lambda i: (i, gate_blk)),
            pl.BlockSpec((tm, d), lambda i: (i, gate_blk + 1)),
            pl.BlockSpec((tm, d), lambda i: (i, 0)),
            pl.BlockSpec((1, 1, d), per_b),
            pl.BlockSpec((BRANCH_W, d), const),
            pl.BlockSpec((BRANCH_W, d), const),
            pl.BlockSpec((d, d), const),
            pl.BlockSpec((1, d), const),
            pl.BlockSpec((1, 1, d), per_b),
            pl.BlockSpec((1, 1, d), per_b),
            pl.BlockSpec((d, qw), const),
        ],
        out_specs=[
            pl.BlockSpec((tm, d), lambda i: (i, 0)),
            pl.BlockSpec((tm, d), lambda i: (i, 0)),
            pl.BlockSpec((tm, qw), lambda i: (i, 0)),
        ],
        out_shape=[
            jax.ShapeDtypeStruct((n, d), F32),
            jax.ShapeDtypeStruct((n, d), BF16),
            jax.ShapeDtypeStruct((n, qw), F32),
        ],
        compiler_params=_cparams(1),
        name="merge",
    )(y_hg, y_gdn, proj, proj, x, gt1, w_bh, w_bg, w_out, g2, sc2, sh2, wq)


def _top_values(s, k):
    rid = lax.broadcasted_iota(jnp.int32, (k, s.shape[1]), 0)
    out = jnp.zeros((k, s.shape[1]), F32)
    for r in range(k):
        m = jnp.max(s, axis=0, keepdims=True)
        out = jnp.where(rid == r, m, out)
        s = jnp.where(s == m, NEG, s)
    return out


def _route_kernel(q_ref, sk_ref, s1_ref, e1_ref, th_ref, e0_ref):
    p = q_ref.shape[0]
    kk = PEER_TOPK
    rid = lax.broadcasted_iota(jnp.int32, (kk, p), 0)
    for h in range(PEER_HEADS):
        q0 = q_ref[:, (2 * h) * PEER_HALF:(2 * h + 1) * PEER_HALF]
        q1 = q_ref[:, (2 * h + 1) * PEER_HALF:(2 * h + 2) * PEER_HALF]
        s0 = _dot_nt(sk_ref[h, 0], q0)
        s1 = _dot_nt(sk_ref[h, 1], q1)
        v0 = _top_values(s0, kk)
        v1 = _top_values(s1, kk)
        cands = []
        for r0 in range(kk):
            valid = rid < (kk // (r0 + 1))
            cands.append(jnp.where(valid, v0[r0:r0 + 1, :] + v1, NEG))
        work = list(cands)
        tops = []
        for _ in range(kk):
            m = work[0]
            for a in work[1:]:
                m = jnp.maximum(m, a)
            m = jnp.max(m, axis=0, keepdims=True)
            tops.append(m)
            work = [jnp.where(a == m, NEG, a) for a in work]
        tau = tops[kk - 1]
        zsum = jnp.zeros_like(tau)
        for tk in tops:
            zsum = zsum + jnp.exp(tk - tops[0])
        th = jnp.full(s0.shape, -NEG, F32)
        for r0 in range(kk):
            sel = cands[r0] >= tau
            theta = jnp.min(jnp.where(sel, v1, -NEG), axis=0, keepdims=True)
            th = jnp.minimum(th, jnp.where(s0 == v0[r0:r0 + 1, :], theta, -NEG))
        s1_ref[h] = s1
        e1_ref[h] = jnp.exp(s1 - v1[0:1, :])
        th_ref[h] = th
        e0_ref[h] = jnp.exp(s0 - v0[0:1, :]) / zsum


def _route(q, subkeys_bf16):
    n, qw = q.shape
    p = 512
    shp = jax.ShapeDtypeStruct((PEER_HEADS, PEER_NKEYS, n), F32)
    spec = pl.BlockSpec((PEER_HEADS, PEER_NKEYS, p), lambda i: (0, 0, i))
    return pl.pallas_call(
        _route_kernel,
        grid=(n // p,),
        in_specs=[
            pl.BlockSpec((p, qw), lambda i: (i, 0)),
            pl.BlockSpec((PEER_HEADS, 2, PEER_NKEYS, PEER_HALF), lambda i: (0, 0, 0, 0)),
        ],
        out_specs=[spec, spec, spec, spec],
        out_shape=[shp, shp, shp, shp],
        compiler_params=_cparams(1),
        name="peer_route",
    )(q, subkeys_bf16)


def _gelu(z):
    return 0.5 * z * (1.0 + lax.erf(z * (2.0 ** -0.5)))


def _experts_kernel(h_ref, u_ref, vt_ref, s1_ref, e1_ref, th_ref, e0_ref, x_ref, gt_ref, o_ref, acc_ref,
                    *, ib):
    step = pl.program_id(1)

    @pl.when(step == 0)
    def _():
        acc_ref[...] = jnp.zeros_like(acc_ref)

    p = h_ref.shape[0]
    hb = h_ref[...]
    for pair in range(ib // 2):
        rows = slice(pair * 2 * PEER_NKEYS, (pair + 1) * 2 * PEER_NKEYS)
        zt = lax.dot_general(u_ref[rows, :], hb, (((1,), (1,)), ((), ())), preferred_element_type=F32)
        gates = []
        for ii in (2 * pair, 2 * pair + 1):
            g = jnp.zeros((PEER_NKEYS, p), F32)
            for h in range(PEER_HEADS):
                th = th_ref[h, ii:ii + 1, :]
                e0 = e0_ref[h, ii:ii + 1, :]
                g = g + e0 * jnp.where(s1_ref[h] >= th, e1_ref[h], 0.0)
            gates.append(g)
        act = (_gelu(zt) * jnp.concatenate(gates, axis=0)).astype(BF16)
        acc_ref[...] += jnp.dot(vt_ref[:, rows], act, preferred_element_type=F32)

    @pl.when(step == pl.num_programs(1) - 1)
    def _():
        o_ref[...] = x_ref[...] + gt_ref[0] * acc_ref[...].T


def _experts(h2, u_bf16, vt_bf16, s1, e1, th, e0, x, gt2, seq):
    n, d = x.shape
    p = 512
    ib = 8
    tpb = seq // p
    blk = ib * PEER_NKEYS
    stat = pl.BlockSpec((PEER_HEADS, PEER_NKEYS, p), lambda i, j: (0, 0, i))
    per_i = pl.BlockSpec((PEER_HEADS, ib, p), lambda i, j: (0, j, i))
    return pl.pallas_call(
        functools.partial(_experts_kernel, ib=ib),
        grid=(n // p, PEER_NKEYS // ib),
        in_specs=[
            pl.BlockSpec((p, d), lambda i, j: (i, 0)),
            pl.BlockSpec((blk, d), lambda i, j: (j, 0)),
            pl.BlockSpec((d, blk), lambda i, j: (0, j)),
            stat, stat, per_i, per_i,
            pl.BlockSpec((p, d), lambda i, j: (i, 0)),
            pl.BlockSpec((1, 1, d), lambda i, j: (i // tpb, 0, 0)),
        ],
        out_specs=pl.BlockSpec((p, d), lambda i, j: (i, 0)),
        out_shape=jax.ShapeDtypeStruct((n, d), F32),
        scratch_shapes=[pltpu.VMEM((d, p), F32)],
        compiler_params=_cparams(2),
        name="peer_experts",
    )(h2, u_bf16, vt_bf16, s1, e1, th, e0, x, gt2)


def _final_kernel(x_ref, g_ref, o_ref):
    o_ref[...] = _rms(x_ref[...], g_ref[...])


def _final_norm(x, g):
    n, d = x.shape
    tm = 1024
    return pl.pallas_call(
        _final_kernel,
        grid=(n // tm,),
        in_specs=[pl.BlockSpec((tm, d), lambda i: (i, 0)), pl.BlockSpec((1, d), lambda i: (0, 0))],
        out_specs=pl.BlockSpec((tm, d), lambda i: (i, 0)),
        out_shape=jax.ShapeDtypeStruct((n, d), F32),
        compiler_params=_cparams(1),
        name="final_norm",
    )(x, g)


def _lane_row(vals_lo, vals_hi):
    row = jnp.zeros((1, LANES), F32)
    row = row.at[0, 0:N_HEADS].set(vals_lo)
    return row.at[0, N_HEADS:2 * N_HEADS].set(vals_hi)


def kernel(x, c, ada_w, ada_b, norm1_g, norm2_g, final_g, w_in, hg_lb_logits, hg_norm_g, gdn_conv_w,
           gdn_a_log, gdn_dt_bias, gdn_norm_g, w_branch_hg, w_branch_gdn, w_out, peer_wq, peer_subkeys,
           peer_u, peer_v):
    batch, seq, d = x.shape
    depth = ada_w.shape[0]
    n = batch * seq

    sm = jax.nn.softmax(hg_lb_logits.astype(F32), axis=0)
    lower_bounds = jnp.cumsum(sm, axis=0) - sm[0:1]

    c_pad = jnp.zeros((SUBLANES, d), F32).at[:batch].set(c)
    mod = _adaln(c_pad, ada_w, ada_b)[:, :batch, :]

    tri = jnp.tril(jnp.ones((CHUNK, CHUNK), F32)).astype(BF16)
    zeros4 = jnp.zeros((N_HEADS,), F32)

    xf = x.reshape(n, d)
    for l in range(depth):
        sh1, sc1, gt1, sh2, sc2, gt2 = [m.reshape(batch, 1, d) for m in jnp.split(mod[l], 6, axis=-1)]
        w_l = w_in[l]
        k0 = 8 * BRANCH_W
        w_main = jnp.concatenate([w_l[:, :k0], w_l[:, k0 + 2 * N_HEADS:]], axis=1).astype(BF16)
        w_small = jnp.zeros((d, LANES), F32).at[:, :2 * N_HEADS].set(w_l[:, k0:k0 + 2 * N_HEADS]).astype(BF16)
        proj, ba = _proj(xf, norm1_g[l].reshape(1, d), sc1, sh1, w_main, w_small, seq)

        y_hg = _hgrn2(proj, lower_bounds[l].reshape(1, BRANCH_W), hg_norm_g[l].reshape(1, HEAD_DIM), tri,
                      batch, seq)
        y_gdn = _gdn(proj, ba, gdn_conv_w[l].astype(F32), _lane_row(zeros4, gdn_a_log[l].astype(F32)),
                     _lane_row(zeros4, gdn_dt_bias[l].astype(F32)), gdn_norm_g[l].reshape(1, HEAD_DIM), tri,
                     batch, seq)

        xf, h2, q = _merge(y_hg, y_gdn, proj, xf, gt1, w_branch_hg[l].astype(BF16),
                           w_branch_gdn[l].astype(BF16), w_out[l].astype(BF16),
                           norm2_g[l].reshape(1, d), sc2, sh2, peer_wq[l].astype(BF16), seq)

        s1, e1, th, e0 = _route(q, peer_subkeys[l].astype(BF16))
        xf = _experts(h2, peer_u[l].astype(BF16), peer_v[l].T.astype(BF16), s1, e1, th, e0, xf, gt2, seq)

    return _final_norm(xf, final_g.reshape(1, d)).reshape(batch, seq, d)
```

```python
import functools
import math

import jax
import jax.numpy as jnp
from jax import lax
from jax.experimental import pallas as pl
from jax.experimental.pallas import tpu as pltpu

F32 = jnp.float32
BF16 = jnp.bfloat16

D_MODEL = 1024
CHUNK = 64
N_HEADS = 4
HEAD_DIM = 128
BRANCH_W = N_HEADS * HEAD_DIM
CONV_K = 4
PEER_HEADS = 8
PEER_NKEYS = 128
PEER_HALF = 128
PEER_TOPK = 16
EPS = 1e-6
NEG = -1e30
EXP_CLAMP = 60.0

LANES = 128
SUBLANES = 8
BF16_ROWS = 16
VMEM_LIMIT = 56 * 1024 * 1024

MAIN_COLS = 8 * BRANCH_W + 2 * D_MODEL


def _cparams(n_axes):
    return pltpu.CompilerParams(
        dimension_semantics=("arbitrary",) * n_axes, vmem_limit_bytes=VMEM_LIMIT)


def _sigmoid(x):
    return 1.0 / (1.0 + jnp.exp(-x))


def _silu(x):
    return x * _sigmoid(x)


def _dot(a, b):
    return jnp.dot(a.astype(BF16), b.astype(BF16), preferred_element_type=F32)


def _dot_nt(a, b):
    return lax.dot_general(a.astype(BF16), b.astype(BF16), (((1,), (1,)), ((), ())),
                           preferred_element_type=F32)


def _dot_tn(a, b):
    return lax.dot_general(a.astype(BF16), b.astype(BF16), (((0,), (0,)), ((), ())),
                           preferred_element_type=F32)


def _mm(a, b):
    return _dot(a, b)


def _dot_split3(m_bf16, x):
    h1 = x.astype(BF16)
    r1 = x - h1.astype(F32)
    h2 = r1.astype(BF16)
    h3 = (r1 - h2.astype(F32)).astype(BF16)
    d = lambda h: jnp.dot(m_bf16, h, preferred_element_type=F32)
    return d(h1) + d(h2) + d(h3)


def _rms(x, g):
    return x * lax.rsqrt(jnp.mean(x * x, axis=-1, keepdims=True) + EPS) * g


def _adaln_kernel(c_ref, w_ref, b_ref, o_ref):
    cond = _silu(c_ref[...])
    o_ref[0] = jnp.dot(cond, w_ref[0], precision=lax.Precision.HIGHEST,
                       preferred_element_type=F32) + b_ref[0]


def _adaln(c_pad, ada_w, ada_b):
    depth, d, cols = ada_w.shape
    tn = 1536
    return pl.pallas_call(
        _adaln_kernel,
        grid=(depth, cols // tn),
        in_specs=[
            pl.BlockSpec((SUBLANES, d), lambda l, j: (0, 0)),
            pl.BlockSpec((1, d, tn), lambda l, j: (l, 0, j)),
            pl.BlockSpec((1, 1, tn), lambda l, j: (l, 0, j)),
        ],
        out_specs=pl.BlockSpec((1, SUBLANES, tn), lambda l, j: (l, 0, j)),
        out_shape=jax.ShapeDtypeStruct((depth, SUBLANES, cols), F32),
        compiler_params=_cparams(2),
        name="adaln",
    )(c_pad, ada_w, ada_b.reshape(depth, 1, cols))


def _proj_kernel(x_ref, g_ref, sc_ref, sh_ref, w_ref, ws_ref, o_ref, os_ref, h_ref):
    @pl.when(pl.program_id(1) == 0)
    def _():
        h = _rms(x_ref[...], g_ref[...]) * (1.0 + sc_ref[0]) + sh_ref[0]
        hb = h.astype(BF16)
        h_ref[...] = hb
        os_ref[...] = jnp.dot(hb, ws_ref[...], preferred_element_type=F32)

    o_ref[...] = jnp.dot(h_ref[...], w_ref[...], preferred_element_type=F32)


def _proj(x, g, sc, sh, w_main, w_small, seq):
    n, d = x.shape
    cols = w_main.shape[1]
    tm, tn = 1024, 1536
    tpb = seq // tm
    return pl.pallas_call(
        _proj_kernel,
        grid=(n // tm, cols // tn),
        in_specs=[
            pl.BlockSpec((tm, d), lambda i, j: (i, 0)),
            pl.BlockSpec((1, d), lambda i, j: (0, 0)),
            pl.BlockSpec((1, 1, d), lambda i, j: (i // tpb, 0, 0)),
            pl.BlockSpec((1, 1, d), lambda i, j: (i // tpb, 0, 0)),
            pl.BlockSpec((d, tn), lambda i, j: (0, j)),
            pl.BlockSpec((d, LANES), lambda i, j: (0, 0)),
        ],
        out_specs=[
            pl.BlockSpec((tm, tn), lambda i, j: (i, j)),
            pl.BlockSpec((tm, LANES), lambda i, j: (i, 0)),
        ],
        out_shape=[
            jax.ShapeDtypeStruct((n, cols), F32),
            jax.ShapeDtypeStruct((n, LANES), F32),
        ],
        scratch_shapes=[pltpu.VMEM((tm, d), BF16)],
        compiler_params=_cparams(2),
        name="in_proj",
    )(x, g, sc, sh, w_main, w_small)


def _hgrn2_kernel(q_ref, f_ref, i_ref, g_ref, lb_ref, ng_ref, tri_ref, y_ref, st_ref, *, n_chunks):
    @pl.when(pl.program_id(1) == 0)
    def _():
        st_ref[...] = jnp.zeros_like(st_ref)

    lb = lb_ref[...]
    ng = ng_ref[...]
    tri = tri_ref[...]
    sub = CHUNK // 4
    row = lax.broadcasted_iota(jnp.int32, (CHUNK, CHUNK), 0)
    col = lax.broadcasted_iota(jnp.int32, (CHUNK, CHUNK), 1)
    causal = col <= row

    probs = [(c, h) for c in range(n_chunks) for h in range(N_HEADS)]
    b_chunks, f_chunks = [], []
    for c in range(n_chunks):
        rows = slice(c * CHUNK, (c + 1) * CHUNK)
        fgate = lb + (1.0 - lb) * _sigmoid(f_ref[rows, :])
        f_chunks.append(fgate)
        b_chunks.append(_dot_split3(tri, jnp.log(fgate)))
    bs, kks, qqs, vvs = [], [], [], []
    for c, h in probs:
        rows = slice(c * CHUNK, (c + 1) * CHUNK)
        hs = slice(h * HEAD_DIM, (h + 1) * HEAD_DIM)
        bs.append(b_chunks[c][:, hs])
        kks.append(1.0 - f_chunks[c][:, hs])
        qqs.append(_silu(q_ref[rows, hs]))
        vvs.append(i_ref[rows, hs])
    atts = []
    for b, kk, qq in zip(bs, kks, qqs):
        att_rows = []
        for blk in range(4):
            r0 = blk * sub
            ref_b = b[r0:r0 + 1, :]
            qe = qq[r0:r0 + sub, :] * jnp.exp(b[r0:r0 + sub, :] - ref_b)
            ke = kk * jnp.exp(jnp.minimum(ref_b - b, EXP_CLAMP))
            att_rows.append(_dot_nt(qe, ke))
        atts.append(jnp.where(causal, jnp.concatenate(att_rows, axis=0), 0.0))
    incs = [_dot_tn(vv, kk * jnp.exp(b[CHUNK - 1:CHUNK, :] - b)) for b, kk, vv in zip(bs, kks, vvs)]
    o_intra = [_dot(att, vv) for att, vv in zip(atts, vvs)]
    states = []
    cur = [st_ref[h] for h in range(N_HEADS)]
    for i, (c, h) in enumerate(probs):
        states.append(cur[h])
        cur[h] = cur[h] * jnp.exp(bs[i][CHUNK - 1:CHUNK, :]) + incs[i]
    for h in range(N_HEADS):
        st_ref[h] = cur[h]
    o_inter = [_dot_nt(qq * jnp.exp(b), st) for qq, b, st in zip(qqs, bs, states)]
    for i, (c, h) in enumerate(probs):
        rows = slice(c * CHUNK, (c + 1) * CHUNK)
        hs = slice(h * HEAD_DIM, (h + 1) * HEAD_DIM)
        y = _rms(o_inter[i] + o_intra[i], ng) * _silu(g_ref[rows, hs])
        y_ref[rows, hs] = y.astype(y_ref.dtype)


def _hgrn2(proj, lb, norm_g, tri, batch, seq):
    t = 256
    spb = seq // t
    col = lambda k: (lambda b, s: (b * spb + s, k))
    const = lambda b, s: (0, 0)
    return pl.pallas_call(
        functools.partial(_hgrn2_kernel, n_chunks=t // CHUNK),
        grid=(batch, spb),
        in_specs=[
            pl.BlockSpec((t, BRANCH_W), col(0)),
            pl.BlockSpec((t, BRANCH_W), col(1)),
            pl.BlockSpec((t, BRANCH_W), col(2)),
            pl.BlockSpec((t, BRANCH_W), col(3)),
            pl.BlockSpec((1, BRANCH_W), const),
            pl.BlockSpec((1, HEAD_DIM), const),
            pl.BlockSpec((CHUNK, CHUNK), const),
        ],
        out_specs=pl.BlockSpec((t, BRANCH_W), lambda b, s: (b * spb + s, 0)),
        out_shape=jax.ShapeDtypeStruct((batch * seq, BRANCH_W), BF16),
        scratch_shapes=[pltpu.VMEM((N_HEADS, HEAD_DIM, HEAD_DIM), F32)],
        compiler_params=_cparams(2),
        name="hgrn2",
    )(proj, proj, proj, proj, lb, norm_g, tri)


def _softplus(x):
    return jnp.maximum(x, 0.0) + jnp.log(1.0 + jnp.exp(-jnp.abs(x)))


def _gdn_kernel(q_ref, k_ref, v_ref, g_ref, ba_ref, cw_ref, alog_ref, dtb_ref, ng_ref, tri_ref,
                y_ref, xb_ref, qn_ref, kn_ref, vn_ref, st_ref, *, t, n_chunks):
    first = pl.program_id(1) == 0

    @pl.when(first)
    def _():
        st_ref[...] = jnp.zeros_like(st_ref)
        xb_ref[:, 0:SUBLANES, :] = jnp.zeros((3, SUBLANES, BRANCH_W), F32)

    @pl.when(jnp.logical_not(first))
    def _():
        xb_ref[:, 0:SUBLANES, :] = xb_ref[:, t:t + SUBLANES, :]

    for j, (src, dst, scale) in enumerate(((q_ref, qn_ref, HEAD_DIM ** -0.5), (k_ref, kn_ref, 1.0),
                                           (v_ref, vn_ref, None))):
        xb_ref[j, SUBLANES:SUBLANES + t, :] = src[...]
        acc = jnp.zeros((t, BRANCH_W), F32)
        for tap in range(CONV_K):
            off = SUBLANES - (CONV_K - 1) + tap
            w = cw_ref[tap:tap + 1, j * BRANCH_W:(j + 1) * BRANCH_W]
            acc = acc + xb_ref[j, off:off + t, :] * w
        acc = _silu(acc)
        if scale is None:
            dst[...] = acc
        else:
            for h in range(N_HEADS):
                hs = slice(h * HEAD_DIM, (h + 1) * HEAD_DIM)
                xh = acc[:, hs]
                dst[:, hs] = xh * (lax.rsqrt(jnp.sum(xh * xh, axis=-1, keepdims=True) + EPS) * scale)

    tri = tri_ref[...]
    ng = ng_ref[...]
    row = lax.broadcasted_iota(jnp.int32, (CHUNK, CHUNK), 0)
    col = lax.broadcasted_iota(jnp.int32, (CHUNK, CHUNK), 1)
    causal = col <= row
    strict = col < row
    eye = (col == row).astype(F32)
    blk16 = (row // 16) == (col // 16)
    blk32 = (row // 32) == (col // 32)

    probs = [(c, h) for c in range(n_chunks) for h in range(N_HEADS)]
    qs, ks, vs, betas, bcols, decs = [], [], [], [], [], []
    for c in range(n_chunks):
        rows = slice(c * CHUNK, (c + 1) * CHUNK)
        ba = ba_ref[rows, :]
        beta_all = _sigmoid(ba)
        logdec = -jnp.exp(alog_ref[...]) * _softplus(ba + dtb_ref[...])
        b_all = _dot_split3(tri, logdec)
        b_pad = jnp.concatenate([b_all, jnp.zeros((LANES - CHUNK, LANES), F32)], axis=0)
        b_t = b_pad.T
        for h in range(N_HEADS):
            hs = slice(h * HEAD_DIM, (h + 1) * HEAD_DIM)
            lane = N_HEADS + h
            bcol = b_all[:, lane:lane + 1]
            brow = b_t[lane:lane + 1, 0:CHUNK]
            bcols.append(bcol)
            betas.append(beta_all[:, h:h + 1])
            decs.append(jnp.exp(jnp.where(causal, bcol - brow, NEG)))
            qs.append(qn_ref[rows, hs])
            ks.append(kn_ref[rows, hs])
            vs.append(vn_ref[rows, hs])
    kbs = [k * beta for k, beta in zip(ks, betas)]
    a_mats = [jnp.where(strict, _dot_nt(kb, k) * dec, 0.0) for kb, k, dec in zip(kbs, ks, decs)]
    aqks = [_dot_nt(q, k) * dec for q, k, dec in zip(qs, ks, decs)]
    a_diags = [jnp.where(blk16, a, 0.0) for a in a_mats]
    ps = [eye - a for a in a_diags]
    xs = [_mm(a, a) for a in a_diags]
    for step in range(3):
        pxs = [_mm(p, x) for p, x in zip(ps, xs)]
        if step < 2:
            xs = [_mm(x, x) for x in xs]
        ps = [p + px for p, px in zip(ps, pxs)]
    for low in ([jnp.where(blk32, a, 0.0) - d for a, d in zip(a_mats, a_diags)],
                [jnp.where(blk32, 0.0, a) for a in a_mats]):
        ms = [_mm(p, lo) for p, lo in zip(ps, low)]
        ps = [p - _mm(m, p) for p, m in zip(ps, ms)]
    rhss = [jnp.concatenate([v * beta, kb * jnp.exp(bcol)], axis=1)
            for v, beta, kb, bcol in zip(vs, betas, kbs, bcols)]
    uws = [rhs + _mm(jnp.where(strict, p, 0.0), rhs) for p, rhs in zip(ps, rhss)]
    wqs = [jnp.concatenate([uw[:, HEAD_DIM:], q * jnp.exp(bcol)], axis=0)
           for uw, q, bcol in zip(uws, qs, bcols)]
    kds = [k * jnp.exp(bcol[CHUNK - 1:CHUNK, :] - bcol) for k, bcol in zip(ks, bcols)]
    cur = [st_ref[h] for h in range(N_HEADS)]
    outs = [None] * len(probs)
    for c in range(n_chunks):
        idx = [c * N_HEADS + h for h in range(N_HEADS)]
        wss = [_dot(wqs[i], cur[h]) for h, i in enumerate(idx)]
        v_news = [uws[i][:, :HEAD_DIM] - ws[:CHUNK] for i, ws in zip(idx, wss)]
        o_new = [_dot(aqks[i], v_new) for i, v_new in zip(idx, v_news)]
        incs = [_dot_tn(kds[i], v_new) for i, v_new in zip(idx, v_news)]
        for h, i in enumerate(idx):
            outs[i] = wss[h][CHUNK:] + o_new[h]
            cur[h] = cur[h] * jnp.exp(bcols[i][CHUNK - 1:CHUNK, :]) + incs[h]
    for h in range(N_HEADS):
        st_ref[h] = cur[h]
    for i, (c, h) in enumerate(probs):
        rows = slice(c * CHUNK, (c + 1) * CHUNK)
        hs = slice(h * HEAD_DIM, (h + 1) * HEAD_DIM)
        y = _rms(outs[i], ng) * _silu(g_ref[rows, hs])
        y_ref[rows, hs] = y.astype(y_ref.dtype)


def _gdn(proj, ba, conv_w, alog_row, dtb_row, norm_g, tri, batch, seq):
    t = 256
    spb = seq // t
    col = lambda k: (lambda b, s: (b * spb + s, k))
    const = lambda b, s: (0, 0)
    return pl.pallas_call(
        functools.partial(_gdn_kernel, t=t, n_chunks=t // CHUNK),
        grid=(batch, spb),
        in_specs=[
            pl.BlockSpec((t, BRANCH_W), col(4)),
            pl.BlockSpec((t, BRANCH_W), col(5)),
            pl.BlockSpec((t, BRANCH_W), col(6)),
            pl.BlockSpec((t, BRANCH_W), col(7)),
            pl.BlockSpec((t, LANES), col(0)),
            pl.BlockSpec((CONV_K, 3 * BRANCH_W), const),
            pl.BlockSpec((1, LANES), const),
            pl.BlockSpec((1, LANES), const),
            pl.BlockSpec((1, HEAD_DIM), const),
            pl.BlockSpec((CHUNK, CHUNK), const),
        ],
        out_specs=pl.BlockSpec((t, BRANCH_W), lambda b, s: (b * spb + s, 0)),
        out_shape=jax.ShapeDtypeStruct((batch * seq, BRANCH_W), BF16),
        scratch_shapes=[
            pltpu.VMEM((3, t + 2 * SUBLANES, BRANCH_W), F32),
            pltpu.VMEM((t, BRANCH_W), F32),
            pltpu.VMEM((t, BRANCH_W), F32),
            pltpu.VMEM((t, BRANCH_W), F32),
            pltpu.VMEM((N_HEADS, HEAD_DIM, HEAD_DIM), F32),
        ],
        compiler_params=_cparams(2),
        name="gdn",
    )(proj, proj, proj, proj, ba, conv_w, alog_row, dtb_row, norm_g, tri)


def _merge_kernel(yh_ref, yg_ref, gh_ref, gg_ref, x_ref, gt_ref, wbh_ref, wbg_ref, wo_ref,
                  g2_ref, sc_ref, sh_ref, wq_ref, xo_ref, h2_ref, q_ref):
    merged = (_sigmoid(gh_ref[...]) * jnp.dot(yh_ref[...], wbh_ref[...], preferred_element_type=F32)
              + _sigmoid(gg_ref[...]) * jnp.dot(yg_ref[...], wbg_ref[...], preferred_element_type=F32))
    y = jnp.dot(merged.astype(BF16), wo_ref[...], preferred_element_type=F32)
    xn = x_ref[...] + gt_ref[0] * y
    xo_ref[...] = xn
    h2 = (_rms(xn, g2_ref[...]) * (1.0 + sc_ref[0]) + sh_ref[0]).astype(BF16)
    h2_ref[...] = h2
    q_ref[...] = jnp.dot(h2, wq_ref[...], preferred_element_type=F32)


def _merge(y_hg, y_gdn, proj, x, gt1, w_bh, w_bg, w_out, g2, sc2, sh2, wq, seq):
    n, d = x.shape
    tm = 512
    tpb = seq // tm
    qw = wq.shape[1]
    const = lambda i: (0, 0)
    per_b = lambda i: (i // tpb, 0, 0)
    gate_blk = 8 * BRANCH_W // D_MODEL
    return pl.pallas_call(
        _merge_kernel,
        grid=(n // tm,),
        in_specs=[
            pl.BlockSpec((tm, BRANCH_W), lambda i: (i, 0)),
            pl.BlockSpec((tm, BRANCH_W), lambda i: (i, 0)),
            pl.BlockSpec((tm, d), lambda i: (i, gate_blk)),
            pl.BlockSpec((tm, d), lambda i: (i, gate_blk + 1)),
            pl.BlockSpec((tm, d), lambda i: (i, 0)),
            pl.BlockSpec((1, 1, d), per_b),
            pl.BlockSpec((BRANCH_W, d), const),
            pl.BlockSpec((BRANCH_W, d), const),
            pl.BlockSpec((d, d), const),
            pl.BlockSpec((1, d), const),
            pl.BlockSpec((1, 1, d), per_b),
            pl.BlockSpec((1, 1, d), per_b),
            pl.BlockSpec((d, qw), const),
        ],
        out_specs=[
            pl.BlockSpec((tm, d), lambda i: (i, 0)),
            pl.BlockSpec((tm, d), lambda i: (i, 0)),
            pl.BlockSpec((tm, qw), lambda i: (i, 0)),
        ],
        out_shape=[
            jax.ShapeDtypeStruct((n, d), F32),
            jax.ShapeDtypeStruct((n, d), BF16),
            jax.ShapeDtypeStruct((n, qw), F32),
        ],
        compiler_params=_cparams(1),
        name="merge",
    )(y_hg, y_gdn, proj, proj, x, gt1, w_bh, w_bg, w_out, g2, sc2, sh2, wq)


def _top_ranked(s, k):
    rid = lax.broadcasted_iota(jnp.int32, (k, s.shape[1]), 0)
    vals = jnp.zeros((k, s.shape[1]), F32)
    rank = jnp.full(s.shape, float(k), F32)
    for r in range(k):
        m = jnp.max(s, axis=0, keepdims=True)
        vals = jnp.where(rid == r, m, vals)
        hit = s == m
        rank = jnp.where(hit, float(r), rank)
        s = jnp.where(hit, NEG, s)
    return vals, rank


def _route_kernel(q_ref, sk_ref, r1_ref, e1_ref, cnt_ref, e0_ref):
    p = q_ref.shape[0]
    kk = PEER_TOPK
    rid = lax.broadcasted_iota(jnp.int32, (kk, p), 0)
    for h in range(PEER_HEADS):
        q0 = q_ref[:, (2 * h) * PEER_HALF:(2 * h + 1) * PEER_HALF]
        q1 = q_ref[:, (2 * h + 1) * PEER_HALF:(2 * h + 2) * PEER_HALF]
        s0 = _dot_nt(sk_ref[h, 0], q0)
        s1 = _dot_nt(sk_ref[h, 1], q1)
        v0, rank0 = _top_ranked(s0, kk)
        v1, rank1 = _top_ranked(s1, kk)
        cands = []
        for r0 in range(kk):
            valid = rid < (kk // (r0 + 1))
            cands.append(jnp.where(valid, v0[r0:r0 + 1, :] + v1, NEG))
        work = list(cands)
        tops = []
        for _ in range(kk):
            m = work[0]
            for a in work[1:]:
                m = jnp.maximum(m, a)
            m = jnp.max(m, axis=0, keepdims=True)
            tops.append(m)
            work = [jnp.where(a == m, NEG, a) for a in work]
        tau = tops[kk - 1]
        zsum = jnp.zeros_like(tau)
        for tk in tops:
            zsum = zsum + jnp.exp(tk - tops[0])
        cnt = jnp.zeros(s0.shape, F32)
        for r0 in range(kk):
            n_sel = jnp.sum(jnp.where(cands[r0] >= tau, 1.0, 0.0), axis=0, keepdims=True)
            cnt = jnp.where(rank0 == float(r0), n_sel, cnt)
        r1_ref[h] = rank1.astype(r1_ref.dtype)
        e1_ref[h] = jnp.exp(s1 - v1[0:1, :]).astype(e1_ref.dtype)
        cnt_ref[h] = cnt
        e0_ref[h] = jnp.exp(s0 - v0[0:1, :]) * (0.5 / zsum)


def _route(q, subkeys_bf16):
    n, qw = q.shape
    p = 512
    shp = lambda dt: jax.ShapeDtypeStruct((PEER_HEADS, PEER_NKEYS, n), dt)
    spec = pl.BlockSpec((PEER_HEADS, PEER_NKEYS, p), lambda i: (0, 0, i))
    return pl.pallas_call(
        _route_kernel,
        grid=(n // p,),
        in_specs=[
            pl.BlockSpec((p, qw), lambda i: (i, 0)),
            pl.BlockSpec((PEER_HEADS, 2, PEER_NKEYS, PEER_HALF), lambda i: (0, 0, 0, 0)),
        ],
        out_specs=[spec, spec, spec, spec],
        out_shape=[shp(BF16), shp(BF16), shp(F32), shp(F32)],
        compiler_params=_cparams(1),
        name="peer_route",
    )(q, subkeys_bf16)


def _experts_kernel(h_ref, u_ref, vt_ref, r1_ref, e1_ref, cnt_ref, e0_ref, x_ref, gt_ref, o_ref,
                    acc_ref, zt_ref, act_ref, *, ib, parts):
    step = pl.program_id(1)

    @pl.when(step == 0)
    def _():
        acc_ref[...] = jnp.zeros_like(acc_ref)

    p = h_ref.shape[0]
    hb = h_ref[...]
    per_part = ib // parts
    width = per_part * PEER_NKEYS
    for part in range(parts):
        rows = slice(part * width, (part + 1) * width)
        zt_ref[rows, :] = lax.dot_general(u_ref[rows, :], hb, (((1,), (1,)), ((), ())),
                                          preferred_element_type=F32)
    for part in range(parts):
        for ii in range(part * per_part, (part + 1) * per_part):
            rows = slice(ii * PEER_NKEYS, (ii + 1) * PEER_NKEYS)
            g = jnp.zeros((PEER_NKEYS // BF16_ROWS, BF16_ROWS, p), BF16)
            for h in range(PEER_HEADS):
                cnt = jnp.broadcast_to(cnt_ref[h, ii:ii + 1, :], (BF16_ROWS, p)).astype(BF16)[None]
                e0 = jnp.broadcast_to(e0_ref[h, ii:ii + 1, :], (BF16_ROWS, p)).astype(BF16)[None]
                r1 = r1_ref[h].reshape(g.shape)
                e1 = e1_ref[h].reshape(g.shape)
                g = g + e0 * jnp.where(r1 < cnt, e1, jnp.zeros((), BF16))
            z = zt_ref[rows, :]
            zg = (z * (1.0 + lax.erf(z * (2.0 ** -0.5)))).astype(BF16)
            act_ref[rows, :] = zg * g.reshape(PEER_NKEYS, p)
        rows = slice(part * width, (part + 1) * width)
        acc_ref[...] += jnp.dot(vt_ref[:, rows], act_ref[rows, :], preferred_element_type=F32)

    @pl.when(step == pl.num_programs(1) - 1)
    def _():
        o_ref[...] = x_ref[...] + gt_ref[0] * acc_ref[...].T


def _experts(h2, u_bf16, vt_bf16, r1, e1, cnt, e0, x, gt2, seq):
    n, d = x.shape
    p = 512
    ib = 8
    tpb = seq // p
    blk = ib * PEER_NKEYS
    stat = pl.BlockSpec((PEER_HEADS, PEER_NKEYS, p), lambda i, j: (0, 0, i))
    per_i = pl.BlockSpec((PEER_HEADS, ib, p), lambda i, j: (0, j, i))
    return pl.pallas_call(
        functools.partial(_experts_kernel, ib=ib, parts=2),
        grid=(n // p, PEER_NKEYS // ib),
        in_specs=[
            pl.BlockSpec((p, d), lambda i, j: (i, 0)),
            pl.BlockSpec((blk, d), lambda i, j: (j, 0)),
            pl.BlockSpec((d, blk), lambda i, j: (0, j)),
            stat, stat, per_i, per_i,
            pl.BlockSpec((p, d), lambda i, j: (i, 0)),
            pl.BlockSpec((1, 1, d), lambda i, j: (i // tpb, 0, 0)),
        ],
        out_specs=pl.BlockSpec((p, d), lambda i, j: (i, 0)),
        out_shape=jax.ShapeDtypeStruct((n, d), F32),
        scratch_shapes=[pltpu.VMEM((d, p), F32), pltpu.VMEM((blk, p), F32), pltpu.VMEM((blk, p), BF16)],
        compiler_params=_cparams(2),
        name="peer_experts",
    )(h2, u_bf16, vt_bf16, r1, e1, cnt, e0, x, gt2)


def _final_kernel(x_ref, g_ref, o_ref):
    o_ref[...] = _rms(x_ref[...], g_ref[...])


def _final_norm(x, g):
    n, d = x.shape
    tm = 1024
    return pl.pallas_call(
        _final_kernel,
        grid=(n // tm,),
        in_specs=[pl.BlockSpec((tm, d), lambda i: (i, 0)), pl.BlockSpec((1, d), lambda i: (0, 0))],
        out_specs=pl.BlockSpec((tm, d), lambda i: (i, 0)),
        out_shape=jax.ShapeDtypeStruct((n, d), F32),
        compiler_params=_cparams(1),
        name="final_norm",
    )(x, g)


def _lane_row(vals_lo, vals_hi):
    row = jnp.zeros((1, LANES), F32)
    row = row.at[0, 0:N_HEADS].set(vals_lo)
    return row.at[0, N_HEADS:2 * N_HEADS].set(vals_hi)


def kernel(x, c, ada_w, ada_b, norm1_g, norm2_g, final_g, w_in, hg_lb_logits, hg_norm_g, gdn_conv_w,
           gdn_a_log, gdn_dt_bias, gdn_norm_g, w_branch_hg, w_branch_gdn, w_out, peer_wq, peer_subkeys,
           peer_u, peer_v):
    batch, seq, d = x.shape
    depth = ada_w.shape[0]
    n = batch * seq

    sm = jax.nn.softmax(hg_lb_logits.astype(F32), axis=0)
    lower_bounds = jnp.cumsum(sm, axis=0) - sm[0:1]

    c_pad = jnp.zeros((SUBLANES, d), F32).at[:batch].set(c)
    mod = _adaln(c_pad, ada_w, ada_b)[:, :batch, :]

    tri = jnp.tril(jnp.ones((CHUNK, CHUNK), F32)).astype(BF16)
    zeros4 = jnp.zeros((N_HEADS,), F32)

    xf = x.reshape(n, d)
    for l in range(depth):
        sh1, sc1, gt1, sh2, sc2, gt2 = [m.reshape(batch, 1, d) for m in jnp.split(mod[l], 6, axis=-1)]
        w_l = w_in[l]
        k0 = 8 * BRANCH_W
        w_main = jnp.concatenate([w_l[:, :k0], w_l[:, k0 + 2 * N_HEADS:]], axis=1).astype(BF16)
        w_small = jnp.zeros((d, LANES), F32).at[:, :2 * N_HEADS].set(w_l[:, k0:k0 + 2 * N_HEADS]).astype(BF16)
        proj, ba = _proj(xf, norm1_g[l].reshape(1, d), sc1, sh1, w_main, w_small, seq)

        y_hg = _hgrn2(proj, lower_bounds[l].reshape(1, BRANCH_W), hg_norm_g[l].reshape(1, HEAD_DIM), tri,
                      batch, seq)
        y_gdn = _gdn(proj, ba, gdn_conv_w[l].astype(F32), _lane_row(zeros4, gdn_a_log[l].astype(F32)),
                     _lane_row(zeros4, gdn_dt_bias[l].astype(F32)), gdn_norm_g[l].reshape(1, HEAD_DIM), tri,
                     batch, seq)

        xf, h2, q = _merge(y_hg, y_gdn, proj, xf, gt1, w_branch_hg[l].astype(BF16),
                           w_branch_gdn[l].astype(BF16), w_out[l].astype(BF16),
                           norm2_g[l].reshape(1, d), sc2, sh2, peer_wq[l].astype(BF16), seq)

        r1, e1, cnt, e0 = _route(q, peer_subkeys[l].astype(BF16))
        xf = _experts(h2, peer_u[l].astype(BF16), peer_v[l].T.astype(BF16), r1, e1, cnt, e0, xf, gt2, seq)

    return _final_norm(xf, final_g.reshape(1, d)).reshape(batch, seq, d)
```

```python
import functools

import jax
import jax.numpy as jnp
from jax import lax
from jax.experimental import pallas as pl
from jax.experimental.pallas import tpu as pltpu

F32 = jnp.float32
BF16 = jnp.bfloat16

D_MODEL = 1024
CHUNK = 64
N_HEADS = 4
HEAD_DIM = 128
BRANCH_W = N_HEADS * HEAD_DIM
CONV_K = 4
PEER_HEADS = 8
PEER_NKEYS = 128
PEER_HALF = 128
PEER_TOPK = 16
EPS = 1e-6
NEG = -1e30
EXP_CLAMP = 60.0
RANK_BASE = 1e30
RANK_STEP = 1e26

LANES = 128
SUBLANES = 8
BF16_ROWS = 16
VMEM_LIMIT = 56 * 1024 * 1024


def _cparams(n_axes):
    return pltpu.CompilerParams(
        dimension_semantics=("arbitrary",) * n_axes, vmem_limit_bytes=VMEM_LIMIT)


def _sigmoid(x):
    return 1.0 / (1.0 + jnp.exp(-x))


def _silu(x):
    return x * _sigmoid(x)


def _dot(a, b):
    return jnp.dot(a.astype(BF16), b.astype(BF16), preferred_element_type=F32)


def _dot_nt(a, b):
    return lax.dot_general(a.astype(BF16), b.astype(BF16), (((1,), (1,)), ((), ())),
                           preferred_element_type=F32)


def _dot_tn(a, b):
    return lax.dot_general(a.astype(BF16), b.astype(BF16), (((0,), (0,)), ((), ())),
                           preferred_element_type=F32)


def _mm(a, b):
    return _dot(a, b)


def _dot_split3(m_bf16, x):
    h1 = x.astype(BF16)
    r1 = x - h1.astype(F32)
    h2 = r1.astype(BF16)
    h3 = (r1 - h2.astype(F32)).astype(BF16)
    d = lambda h: jnp.dot(m_bf16, h, preferred_element_type=F32)
    return d(h1) + d(h2) + d(h3)


def _rms(x, g):
    return x * lax.rsqrt(jnp.mean(x * x, axis=-1, keepdims=True) + EPS) * g


def _adaln_kernel(c_ref, w_ref, b_ref, o_ref):
    cond = _silu(c_ref[...])
    o_ref[0] = jnp.dot(cond, w_ref[0], precision=lax.Precision.HIGHEST,
                       preferred_element_type=F32) + b_ref[0]


def _adaln(c_pad, ada_w, ada_b):
    depth, d, cols = ada_w.shape
    tn = 1536
    return pl.pallas_call(
        _adaln_kernel,
        grid=(depth, cols // tn),
        in_specs=[
            pl.BlockSpec((SUBLANES, d), lambda l, j: (0, 0)),
            pl.BlockSpec((1, d, tn), lambda l, j: (l, 0, j)),
            pl.BlockSpec((1, 1, tn), lambda l, j: (l, 0, j)),
        ],
        out_specs=pl.BlockSpec((1, SUBLANES, tn), lambda l, j: (l, 0, j)),
        out_shape=jax.ShapeDtypeStruct((depth, SUBLANES, cols), F32),
        compiler_params=_cparams(2),
        name="adaln",
    )(c_pad, ada_w, ada_b.reshape(depth, 1, cols))


def _proj_kernel(x_ref, g_ref, sc_ref, sh_ref, w_ref, ws_ref, o_ref, os_ref, h_ref):
    @pl.when(pl.program_id(1) == 0)
    def _():
        h = _rms(x_ref[...], g_ref[...]) * (1.0 + sc_ref[0]) + sh_ref[0]
        hb = h.astype(BF16)
        h_ref[...] = hb
        os_ref[...] = jnp.dot(hb, ws_ref[...], preferred_element_type=F32)

    o_ref[...] = jnp.dot(h_ref[...], w_ref[...], preferred_element_type=F32)


def _proj(x, g, sc, sh, w_main, w_small, seq):
    n, d = x.shape
    cols = w_main.shape[1]
    tm, tn = 1024, 1536
    tpb = seq // tm
    return pl.pallas_call(
        _proj_kernel,
        grid=(n // tm, cols // tn),
        in_specs=[
            pl.BlockSpec((tm, d), lambda i, j: (i, 0)),
            pl.BlockSpec((1, d), lambda i, j: (0, 0)),
            pl.BlockSpec((1, 1, d), lambda i, j: (i // tpb, 0, 0)),
            pl.BlockSpec((1, 1, d), lambda i, j: (i // tpb, 0, 0)),
            pl.BlockSpec((d, tn), lambda i, j: (0, j)),
            pl.BlockSpec((d, LANES), lambda i, j: (0, 0)),
        ],
        out_specs=[
            pl.BlockSpec((tm, tn), lambda i, j: (i, j)),
            pl.BlockSpec((tm, LANES), lambda i, j: (i, 0)),
        ],
        out_shape=[
            jax.ShapeDtypeStruct((n, cols), F32),
            jax.ShapeDtypeStruct((n, LANES), F32),
        ],
        scratch_shapes=[pltpu.VMEM((tm, d), BF16)],
        compiler_params=_cparams(2),
        name="in_proj",
    )(x, g, sc, sh, w_main, w_small)


def _hgrn2_kernel(q_ref, f_ref, i_ref, g_ref, lb_ref, ng_ref, tri_ref, y_ref, st_ref, *, n_chunks):
    @pl.when(pl.program_id(1) == 0)
    def _():
        st_ref[...] = jnp.zeros_like(st_ref)

    lb = lb_ref[...]
    ng = ng_ref[...]
    tri = tri_ref[...]
    sub = CHUNK // 4
    row = lax.broadcasted_iota(jnp.int32, (CHUNK, CHUNK), 0)
    col = lax.broadcasted_iota(jnp.int32, (CHUNK, CHUNK), 1)
    causal = col <= row

    probs = [(c, h) for c in range(n_chunks) for h in range(N_HEADS)]
    b_chunks, f_chunks = [], []
    for c in range(n_chunks):
        rows = slice(c * CHUNK, (c + 1) * CHUNK)
        fgate = lb + (1.0 - lb) * _sigmoid(f_ref[rows, :])
        f_chunks.append(fgate)
        b_chunks.append(_dot_split3(tri, jnp.log(fgate)))
    bs, kks, qqs, vvs = [], [], [], []
    for c, h in probs:
        rows = slice(c * CHUNK, (c + 1) * CHUNK)
        hs = slice(h * HEAD_DIM, (h + 1) * HEAD_DIM)
        bs.append(b_chunks[c][:, hs])
        kks.append(1.0 - f_chunks[c][:, hs])
        qqs.append(_silu(q_ref[rows, hs]))
        vvs.append(i_ref[rows, hs])
    atts = []
    for b, kk, qq in zip(bs, kks, qqs):
        att_rows = []
        for blk in range(4):
            r0 = blk * sub
            ref_b = b[r0:r0 + 1, :]
            qe = qq[r0:r0 + sub, :] * jnp.exp(b[r0:r0 + sub, :] - ref_b)
            ke = kk * jnp.exp(jnp.minimum(ref_b - b, EXP_CLAMP))
            att_rows.append(_dot_nt(qe, ke))
        atts.append(jnp.where(causal, jnp.concatenate(att_rows, axis=0), 0.0))
    incs = [_dot_tn(vv, kk * jnp.exp(b[CHUNK - 1:CHUNK, :] - b)) for b, kk, vv in zip(bs, kks, vvs)]
    o_intra = [_dot(att, vv) for att, vv in zip(atts, vvs)]
    states = []
    cur = [st_ref[h] for h in range(N_HEADS)]
    for i, (c, h) in enumerate(probs):
        states.append(cur[h])
        cur[h] = cur[h] * jnp.exp(bs[i][CHUNK - 1:CHUNK, :]) + incs[i]
    for h in range(N_HEADS):
        st_ref[h] = cur[h]
    o_inter = [_dot_nt(qq * jnp.exp(b), st) for qq, b, st in zip(qqs, bs, states)]
    for i, (c, h) in enumerate(probs):
        rows = slice(c * CHUNK, (c + 1) * CHUNK)
        hs = slice(h * HEAD_DIM, (h + 1) * HEAD_DIM)
        y = _rms(o_inter[i] + o_intra[i], ng) * _silu(g_ref[rows, hs])
        y_ref[rows, hs] = y.astype(y_ref.dtype)


def _hgrn2(proj, lb, norm_g, tri, batch, seq):
    t = 256
    spb = seq // t
    col = lambda k: (lambda b, s: (b * spb + s, k))
    const = lambda b, s: (0, 0)
    return pl.pallas_call(
        functools.partial(_hgrn2_kernel, n_chunks=t // CHUNK),
        grid=(batch, spb),
        in_specs=[
            pl.BlockSpec((t, BRANCH_W), col(0)),
            pl.BlockSpec((t, BRANCH_W), col(1)),
            pl.BlockSpec((t, BRANCH_W), col(2)),
            pl.BlockSpec((t, BRANCH_W), col(3)),
            pl.BlockSpec((1, BRANCH_W), const),
            pl.BlockSpec((1, HEAD_DIM), const),
            pl.BlockSpec((CHUNK, CHUNK), const),
        ],
        out_specs=pl.BlockSpec((t, BRANCH_W), lambda b, s: (b * spb + s, 0)),
        out_shape=jax.ShapeDtypeStruct((batch * seq, BRANCH_W), BF16),
        scratch_shapes=[pltpu.VMEM((N_HEADS, HEAD_DIM, HEAD_DIM), F32)],
        compiler_params=_cparams(2),
        name="hgrn2",
    )(proj, proj, proj, proj, lb, norm_g, tri)


def _softplus(x):
    return jnp.maximum(x, 0.0) + jnp.log(1.0 + jnp.exp(-jnp.abs(x)))


def _gdn_kernel(q_ref, k_ref, v_ref, g_ref, ba_ref, cw_ref, alog_ref, dtb_ref, ng_ref, tri_ref,
                y_ref, xb_ref, qn_ref, kn_ref, vn_ref, st_ref, *, t, n_chunks):
    first = pl.program_id(1) == 0

    @pl.when(first)
    def _():
        st_ref[...] = jnp.zeros_like(st_ref)
        xb_ref[:, 0:SUBLANES, :] = jnp.zeros((3, SUBLANES, BRANCH_W), F32)

    @pl.when(jnp.logical_not(first))
    def _():
        xb_ref[:, 0:SUBLANES, :] = xb_ref[:, t:t + SUBLANES, :]

    for j, (src, dst, scale) in enumerate(((q_ref, qn_ref, HEAD_DIM ** -0.5), (k_ref, kn_ref, 1.0),
                                           (v_ref, vn_ref, None))):
        xb_ref[j, SUBLANES:SUBLANES + t, :] = src[...]
        acc = jnp.zeros((t, BRANCH_W), F32)
        for tap in range(CONV_K):
            off = SUBLANES - (CONV_K - 1) + tap
            w = cw_ref[tap:tap + 1, j * BRANCH_W:(j + 1) * BRANCH_W]
            acc = acc + xb_ref[j, off:off + t, :] * w
        acc = _silu(acc)
        if scale is None:
            dst[...] = acc
        else:
            for h in range(N_HEADS):
                hs = slice(h * HEAD_DIM, (h + 1) * HEAD_DIM)
                xh = acc[:, hs]
                dst[:, hs] = xh * (lax.rsqrt(jnp.sum(xh * xh, axis=-1, keepdims=True) + EPS) * scale)

    tri = tri_ref[...]
    ng = ng_ref[...]
    row = lax.broadcasted_iota(jnp.int32, (CHUNK, CHUNK), 0)
    col = lax.broadcasted_iota(jnp.int32, (CHUNK, CHUNK), 1)
    causal = col <= row
    strict = col < row
    eye = (col == row).astype(F32)
    blk16 = (row // 16) == (col // 16)
    blk32 = (row // 32) == (col // 32)

    probs = [(c, h) for c in range(n_chunks) for h in range(N_HEADS)]
    qs, ks, vs, betas, bcols, decs = [], [], [], [], [], []
    for c in range(n_chunks):
        rows = slice(c * CHUNK, (c + 1) * CHUNK)
        ba = ba_ref[rows, :]
        beta_all = _sigmoid(ba)
        logdec = -jnp.exp(alog_ref[...]) * _softplus(ba + dtb_ref[...])
        b_all = _dot_split3(tri, logdec)
        b_pad = jnp.concatenate([b_all, jnp.zeros((LANES - CHUNK, LANES), F32)], axis=0)
        b_t = b_pad.T
        for h in range(N_HEADS):
            hs = slice(h * HEAD_DIM, (h + 1) * HEAD_DIM)
            lane = N_HEADS + h
            bcol = b_all[:, lane:lane + 1]
            brow = b_t[lane:lane + 1, 0:CHUNK]
            bcols.append(bcol)
            betas.append(beta_all[:, h:h + 1])
            decs.append(jnp.exp(jnp.where(causal, bcol - brow, NEG)))
            qs.append(qn_ref[rows, hs])
            ks.append(kn_ref[rows, hs])
            vs.append(vn_ref[rows, hs])
    kbs = [k * beta for k, beta in zip(ks, betas)]
    a_mats = [jnp.where(strict, _dot_nt(kb, k) * dec, 0.0) for kb, k, dec in zip(kbs, ks, decs)]
    aqks = [_dot_nt(q, k) * dec for q, k, dec in zip(qs, ks, decs)]
    a_diags = [jnp.where(blk16, a, 0.0) for a in a_mats]
    ps = [eye - a for a in a_diags]
    xs = [_mm(a, a) for a in a_diags]
    for step in range(3):
        pxs = [_mm(p, x) for p, x in zip(ps, xs)]
        if step < 2:
            xs = [_mm(x, x) for x in xs]
        ps = [p + px for p, px in zip(ps, pxs)]
    for low in ([jnp.where(blk32, a, 0.0) - d for a, d in zip(a_mats, a_diags)],
                [jnp.where(blk32, 0.0, a) for a in a_mats]):
        ms = [_mm(p, lo) for p, lo in zip(ps, low)]
        ps = [p - _mm(m, p) for p, m in zip(ps, ms)]
    rhss = [jnp.concatenate([v * beta, kb * jnp.exp(bcol)], axis=1)
            for v, beta, kb, bcol in zip(vs, betas, kbs, bcols)]
    uws = [rhs + _mm(jnp.where(strict, p, 0.0), rhs) for p, rhs in zip(ps, rhss)]
    wqs = [jnp.concatenate([uw[:, HEAD_DIM:], q * jnp.exp(bcol)], axis=0)
           for uw, q, bcol in zip(uws, qs, bcols)]
    kds = [k * jnp.exp(bcol[CHUNK - 1:CHUNK, :] - bcol) for k, bcol in zip(ks, bcols)]
    cur = [st_ref[h] for h in range(N_HEADS)]
    outs = [None] * len(probs)
    for c in range(n_chunks):
        idx = [c * N_HEADS + h for h in range(N_HEADS)]
        wss = [_dot(wqs[i], cur[h]) for h, i in enumerate(idx)]
        v_news = [uws[i][:, :HEAD_DIM] - ws[:CHUNK] for i, ws in zip(idx, wss)]
        o_new = [_dot(aqks[i], v_new) for i, v_new in zip(idx, v_news)]
        incs = [_dot_tn(kds[i], v_new) for i, v_new in zip(idx, v_news)]
        for h, i in enumerate(idx):
            outs[i] = wss[h][CHUNK:] + o_new[h]
            cur[h] = cur[h] * jnp.exp(bcols[i][CHUNK - 1:CHUNK, :]) + incs[h]
    for h in range(N_HEADS):
        st_ref[h] = cur[h]
    for i, (c, h) in enumerate(probs):
        rows = slice(c * CHUNK, (c + 1) * CHUNK)
        hs = slice(h * HEAD_DIM, (h + 1) * HEAD_DIM)
        y = _rms(outs[i], ng) * _silu(g_ref[rows, hs])
        y_ref[rows, hs] = y.astype(y_ref.dtype)


def _gdn(proj, ba, conv_w, alog_row, dtb_row, norm_g, tri, batch, seq):
    t = 256
    spb = seq // t
    col = lambda k: (lambda b, s: (b * spb + s, k))
    const = lambda b, s: (0, 0)
    return pl.pallas_call(
        functools.partial(_gdn_kernel, t=t, n_chunks=t // CHUNK),
        grid=(batch, spb),
        in_specs=[
            pl.BlockSpec((t, BRANCH_W), col(4)),
            pl.BlockSpec((t, BRANCH_W), col(5)),
            pl.BlockSpec((t, BRANCH_W), col(6)),
            pl.BlockSpec((t, BRANCH_W), col(7)),
            pl.BlockSpec((t, LANES), col(0)),
            pl.BlockSpec((CONV_K, 3 * BRANCH_W), const),
            pl.BlockSpec((1, LANES), const),
            pl.BlockSpec((1, LANES), const),
            pl.BlockSpec((1, HEAD_DIM), const),
            pl.BlockSpec((CHUNK, CHUNK), const),
        ],
        out_specs=pl.BlockSpec((t, BRANCH_W), lambda b, s: (b * spb + s, 0)),
        out_shape=jax.ShapeDtypeStruct((batch * seq, BRANCH_W), BF16),
        scratch_shapes=[
            pltpu.VMEM((3, t + 2 * SUBLANES, BRANCH_W), F32),
            pltpu.VMEM((t, BRANCH_W), F32),
            pltpu.VMEM((t, BRANCH_W), F32),
            pltpu.VMEM((t, BRANCH_W), F32),
            pltpu.VMEM((N_HEADS, HEAD_DIM, HEAD_DIM), F32),
        ],
        compiler_params=_cparams(2),
        name="gdn",
    )(proj, proj, proj, proj, ba, conv_w, alog_row, dtb_row, norm_g, tri)


def _merge_kernel(yh_ref, yg_ref, gh_ref, gg_ref, x_ref, gt_ref, wbh_ref, wbg_ref, wo_ref,
                  g2_ref, sc_ref, sh_ref, wq_ref, xo_ref, h2_ref, q_ref):
    merged = (_sigmoid(gh_ref[...]) * jnp.dot(yh_ref[...], wbh_ref[...], preferred_element_type=F32)
              + _sigmoid(gg_ref[...]) * jnp.dot(yg_ref[...], wbg_ref[...], preferred_element_type=F32))
    y = jnp.dot(merged.astype(BF16), wo_ref[...], preferred_element_type=F32)
    xn = x_ref[...] + gt_ref[0] * y
    xo_ref[...] = xn
    h2 = (_rms(xn, g2_ref[...]) * (1.0 + sc_ref[0]) + sh_ref[0]).astype(BF16)
    h2_ref[...] = pltpu.bitcast(h2, jnp.uint32)
    q_ref[...] = jnp.dot(h2, wq_ref[...], preferred_element_type=F32)


def _merge(y_hg, y_gdn, proj, x, gt1, w_bh, w_bg, w_out, g2, sc2, sh2, wq, seq):
    n, d = x.shape
    tm = 512
    tpb = seq // tm
    qw = wq.shape[1]
    const = lambda i: (0, 0)
    per_b = lambda i: (i // tpb, 0, 0)
    gate_blk = 8 * BRANCH_W // D_MODEL
    return pl.pallas_call(
        _merge_kernel,
        grid=(n // tm,),
        in_specs=[
            pl.BlockSpec((tm, BRANCH_W), lambda i: (i, 0)),
            pl.BlockSpec((tm, BRANCH_W), lambda i: (i, 0)),
            pl.BlockSpec((tm, d), lambda i: (i, gate_blk)),
            pl.BlockSpec((tm, d), lambda i: (i, gate_blk + 1)),
            pl.BlockSpec((tm, d), lambda i: (i, 0)),
            pl.BlockSpec((1, 1, d), per_b),
            pl.BlockSpec((BRANCH_W, d), const),
            pl.BlockSpec((BRANCH_W, d), const),
            pl.BlockSpec((d, d), const),
            pl.BlockSpec((1, d), const),
            pl.BlockSpec((1, 1, d), per_b),
            pl.BlockSpec((1, 1, d), per_b),
            pl.BlockSpec((d, qw), const),
        ],
        out_specs=[
            pl.BlockSpec((tm, d), lambda i: (i, 0)),
            pl.BlockSpec((tm // 2, d), lambda i: (i, 0)),
            pl.BlockSpec((tm, qw), lambda i: (i, 0)),
        ],
        out_shape=[
            jax.ShapeDtypeStruct((n, d), F32),
            jax.ShapeDtypeStruct((n // 2, d), jnp.uint32),
            jax.ShapeDtypeStruct((n, qw), F32),
        ],
        compiler_params=_cparams(1),
        name="merge",
    )(y_hg, y_gdn, proj, proj, x, gt1, w_bh, w_bg, w_out, g2, sc2, sh2, wq)


def _top_ranked(s, k):
    rid = lax.broadcasted_iota(jnp.int32, (k, s.shape[1]), 0)
    vals = jnp.zeros((k, s.shape[1]), F32)
    for r in range(k):
        m = jnp.max(s, axis=0, keepdims=True)
        vals = jnp.where(rid == r, m, vals)
        s = jnp.where(s == m, -(RANK_BASE + r * RANK_STEP), s)
    rank = jnp.where(s < -0.5 * RANK_BASE, jnp.round((-s - RANK_BASE) * (1.0 / RANK_STEP)), float(k))
    return vals, rank


def _route_kernel(q_ref, sk_ref, r1_ref, e1_ref, cnt_ref, e0_ref):
    p = q_ref.shape[0]
    kk = PEER_TOPK
    rid = lax.broadcasted_iota(jnp.int32, (SUBLANES, p), 0)
    for h in range(PEER_HEADS):
        q0 = q_ref[:, (2 * h) * PEER_HALF:(2 * h + 1) * PEER_HALF]
        q1 = q_ref[:, (2 * h + 1) * PEER_HALF:(2 * h + 2) * PEER_HALF]
        s0 = _dot_nt(sk_ref[h, 0], q0)
        s1 = _dot_nt(sk_ref[h, 1], q1)
        v0, rank0 = _top_ranked(s0, kk)
        v1, rank1 = _top_ranked(s1, kk)
        lo, hi = v1[0:SUBLANES, :], v1[SUBLANES:kk, :]
        pieces = [v0[0:1, :] + lo, v0[0:1, :] + hi, v0[1:2, :] + lo]
        for r0 in range(2, SUBLANES):
            pieces.append(jnp.where(rid < kk // (r0 + 1), v0[r0:r0 + 1, :] + lo, NEG))
        pieces.append(v0[SUBLANES:kk, :] + v1[0:1, :])
        work = list(pieces)
        tops = []
        for _ in range(kk):
            m = work[0]
            for a in work[1:]:
                m = jnp.maximum(m, a)
            m = jnp.max(m, axis=0, keepdims=True)
            tops.append(m)
            work = [jnp.where(a == m, NEG, a) for a in work]
        tau = tops[kk - 1]
        zsum = jnp.zeros_like(tau)
        for tk in tops:
            zsum = zsum + jnp.exp(tk - tops[0])
        sel = [jnp.where(pc >= tau, 1.0, 0.0) for pc in pieces]
        n_sel = [jnp.sum(sel[0] + sel[1], axis=0, keepdims=True)]
        n_sel += [jnp.sum(sel[r0 + 1], axis=0, keepdims=True) for r0 in range(1, SUBLANES)]
        n_high = jnp.sum(sel[SUBLANES + 1], axis=0, keepdims=True)
        high = (rank0 >= float(SUBLANES)) & (rank0 < float(SUBLANES) + n_high)
        cnt = jnp.where(high, 1.0, 0.0)
        for r0 in range(SUBLANES):
            cnt = jnp.where(rank0 == float(r0), n_sel[r0], cnt)
        r1_ref[h] = pltpu.bitcast(rank1.astype(BF16), jnp.uint32)
        e1_ref[h] = pltpu.bitcast(jnp.exp(s1 - v1[0:1, :]).astype(BF16), jnp.uint32)
        cnt_ref[h] = cnt
        e0_ref[h] = jnp.exp(s0 - v0[0:1, :]) * (0.5 / zsum)


def _route(q, subkeys_bf16):
    n, qw = q.shape
    p = LANES
    shp = jax.ShapeDtypeStruct((PEER_HEADS, PEER_NKEYS, n), F32)
    spec = pl.BlockSpec((PEER_HEADS, PEER_NKEYS, p), lambda i: (0, 0, i))
    shp_packed = jax.ShapeDtypeStruct((PEER_HEADS, PEER_NKEYS // 2, n), jnp.uint32)
    spec_packed = pl.BlockSpec((PEER_HEADS, PEER_NKEYS // 2, p), lambda i: (0, 0, i))
    return pl.pallas_call(
        _route_kernel,
        grid=(n // p,),
        in_specs=[
            pl.BlockSpec((p, qw), lambda i: (i, 0)),
            pl.BlockSpec((PEER_HEADS, 2, PEER_NKEYS, PEER_HALF), lambda i: (0, 0, 0, 0)),
        ],
        out_specs=[spec_packed, spec_packed, spec, spec],
        out_shape=[shp_packed, shp_packed, shp, shp],
        compiler_params=_cparams(1),
        name="peer_route",
    )(q, subkeys_bf16)


def _experts_kernel(h_ref, u_ref, vt_ref, r1_ref, e1_ref, cnt_ref, e0_ref, x_ref, gt_ref, o_ref,
                    acc_ref, zt_ref, act_ref, *, ib, parts):
    step = pl.program_id(1)

    @pl.when(step == 0)
    def _():
        acc_ref[...] = jnp.zeros_like(acc_ref)

    hb = pltpu.bitcast(h_ref[...], BF16)
    p = hb.shape[0]
    per_part = ib // parts
    width = per_part * PEER_NKEYS
    for part in range(parts):
        rows = slice(part * width, (part + 1) * width)
        u_part = pltpu.bitcast(u_ref[part * width // 2:(part + 1) * width // 2, :], BF16)
        zt_ref[rows, :] = lax.dot_general(u_part, hb, (((1,), (1,)), ((), ())), preferred_element_type=F32)
    for part in range(parts):
        iis = range(part * per_part, (part + 1) * per_part)
        for lt in range(p // LANES):
            ls = slice(lt * LANES, (lt + 1) * LANES)
            gs = [jnp.zeros((PEER_NKEYS // BF16_ROWS, BF16_ROWS, LANES), BF16) for _ in iis]
            for h in range(PEER_HEADS):
                r1 = pltpu.bitcast(r1_ref[h, :, ls], BF16).reshape(gs[0].shape)
                e1 = pltpu.bitcast(e1_ref[h, :, ls], BF16).reshape(gs[0].shape)
                for k, ii in enumerate(iis):
                    cnt = jnp.broadcast_to(cnt_ref[h, ii:ii + 1, ls], (BF16_ROWS, LANES)).astype(BF16)[None]
                    e0 = jnp.broadcast_to(e0_ref[h, ii:ii + 1, ls], (BF16_ROWS, LANES)).astype(BF16)[None]
                    gs[k] = gs[k] + e0 * jnp.where(r1 < cnt, e1, jnp.zeros((), BF16))
            for k, ii in enumerate(iis):
                rows = slice(ii * PEER_NKEYS, (ii + 1) * PEER_NKEYS)
                z = zt_ref[rows, ls]
                zg = (z * (1.0 + lax.erf(z * (2.0 ** -0.5)))).astype(BF16)
                act_ref[rows, ls] = zg * gs[k].reshape(PEER_NKEYS, LANES)
        rows = slice(part * width, (part + 1) * width)
        acc_ref[...] += jnp.dot(pltpu.bitcast(vt_ref[:, rows], BF16), act_ref[rows, :],
                                preferred_element_type=F32)

    @pl.when(step == pl.num_programs(1) - 1)
    def _():
        o_ref[...] = x_ref[...] + gt_ref[0] * acc_ref[...].T


def _experts(h2, u_packed, vt_packed, r1, e1, cnt, e0, x, gt2, seq):
    n, d = x.shape
    p = 512
    ib = 8
    tpb = seq // p
    blk = ib * PEER_NKEYS
    stat = pl.BlockSpec((PEER_HEADS, PEER_NKEYS // 2, p), lambda i, j: (0, 0, i))
    per_i = pl.BlockSpec((PEER_HEADS, ib, p), lambda i, j: (0, j, i))
    return pl.pallas_call(
        functools.partial(_experts_kernel, ib=ib, parts=4),
        grid=(n // p, PEER_NKEYS // ib),
        in_specs=[
            pl.BlockSpec((p // 2, d), lambda i, j: (i, 0)),
            pl.BlockSpec((blk // 2, d), lambda i, j: (j, 0)),
            pl.BlockSpec((d // 2, blk), lambda i, j: (0, j)),
            stat, stat, per_i, per_i,
            pl.BlockSpec((p, d), lambda i, j: (i, 0)),
            pl.BlockSpec((1, 1, d), lambda i, j: (i // tpb, 0, 0)),
        ],
        out_specs=pl.BlockSpec((p, d), lambda i, j: (i, 0)),
        out_shape=jax.ShapeDtypeStruct((n, d), F32),
        scratch_shapes=[pltpu.VMEM((d, p), F32), pltpu.VMEM((blk, p), F32), pltpu.VMEM((blk, p), BF16)],
        compiler_params=_cparams(2),
        name="peer_experts",
    )(h2, u_packed, vt_packed, r1, e1, cnt, e0, x, gt2)


def _final_kernel(x_ref, g_ref, o_ref):
    o_ref[...] = _rms(x_ref[...], g_ref[...])


def _final_norm(x, g):
    n, d = x.shape
    tm = 1024
    return pl.pallas_call(
        _final_kernel,
        grid=(n // tm,),
        in_specs=[pl.BlockSpec((tm, d), lambda i: (i, 0)), pl.BlockSpec((1, d), lambda i: (0, 0))],
        out_specs=pl.BlockSpec((tm, d), lambda i: (i, 0)),
        out_shape=jax.ShapeDtypeStruct((n, d), F32),
        compiler_params=_cparams(1),
        name="final_norm",
    )(x, g)


def _pack_rows(x):
    r, c = x.shape
    return lax.bitcast_convert_type(jnp.swapaxes(x.reshape(r // 2, 2, c), -1, -2), jnp.uint32)


def _lane_row(vals_lo, vals_hi):
    row = jnp.zeros((1, LANES), F32)
    row = row.at[0, 0:N_HEADS].set(vals_lo)
    return row.at[0, N_HEADS:2 * N_HEADS].set(vals_hi)


def kernel(x, c, ada_w, ada_b, norm1_g, norm2_g, final_g, w_in, hg_lb_logits, hg_norm_g, gdn_conv_w,
           gdn_a_log, gdn_dt_bias, gdn_norm_g, w_branch_hg, w_branch_gdn, w_out, peer_wq, peer_subkeys,
           peer_u, peer_v):
    batch, seq, d = x.shape
    depth = ada_w.shape[0]
    n = batch * seq

    sm = jax.nn.softmax(hg_lb_logits.astype(F32), axis=0)
    lower_bounds = jnp.cumsum(sm, axis=0) - sm[0:1]

    c_pad = jnp.zeros((SUBLANES, d), F32).at[:batch].set(c)
    mod = _adaln(c_pad, ada_w, ada_b)[:, :batch, :]

    tri = jnp.tril(jnp.ones((CHUNK, CHUNK), F32)).astype(BF16)
    zeros4 = jnp.zeros((N_HEADS,), F32)

    xf = x.reshape(n, d)
    for l in range(depth):
        sh1, sc1, gt1, sh2, sc2, gt2 = [m.reshape(batch, 1, d) for m in jnp.split(mod[l], 6, axis=-1)]
        w_l = w_in[l]
        k0 = 8 * BRANCH_W
        w_main = jnp.concatenate([w_l[:, :k0], w_l[:, k0 + 2 * N_HEADS:]], axis=1).astype(BF16)
        w_small = jnp.zeros((d, LANES), F32).at[:, :2 * N_HEADS].set(w_l[:, k0:k0 + 2 * N_HEADS]).astype(BF16)
        proj, ba = _proj(xf, norm1_g[l].reshape(1, d), sc1, sh1, w_main, w_small, seq)

        y_hg = _hgrn2(proj, lower_bounds[l].reshape(1, BRANCH_W), hg_norm_g[l].reshape(1, HEAD_DIM), tri,
                      batch, seq)
        y_gdn = _gdn(proj, ba, gdn_conv_w[l].astype(F32), _lane_row(zeros4, gdn_a_log[l].astype(F32)),
                     _lane_row(zeros4, gdn_dt_bias[l].astype(F32)), gdn_norm_g[l].reshape(1, HEAD_DIM), tri,
                     batch, seq)

        xf, h2, q = _merge(y_hg, y_gdn, proj, xf, gt1, w_branch_hg[l].astype(BF16),
                           w_branch_gdn[l].astype(BF16), w_out[l].astype(BF16),
                           norm2_g[l].reshape(1, d), sc2, sh2, peer_wq[l].astype(BF16), seq)

        r1, e1, cnt, e0 = _route(q, peer_subkeys[l].astype(BF16))
        xf = _experts(h2, _pack_rows(peer_u[l].astype(BF16)), _pack_rows(peer_v[l].astype(BF16).T),
                      r1, e1, cnt, e0, xf, gt2, seq)

    return _final_norm(xf, final_g.reshape(1, d)).reshape(batch, seq, d)
```

```python
import functools

import jax
import jax.numpy as jnp
from jax import lax
from jax.experimental import pallas as pl
from jax.experimental.pallas import tpu as pltpu

F32 = jnp.float32
BF16 = jnp.bfloat16

D_MODEL = 1024
CHUNK = 64
N_HEADS = 4
HEAD_DIM = 128
BRANCH_W = N_HEADS * HEAD_DIM
CONV_K = 4
PEER_HEADS = 8
PEER_NKEYS = 128
PEER_HALF = 128
PEER_TOPK = 16
EPS = 1e-6
NEG = -1e30
EXP_CLAMP = 60.0
RANK_BASE = 1e30
RANK_STEP = 1e26

LANES = 128
SUBLANES = 8
BF16_ROWS = 16
VMEM_LIMIT = 56 * 1024 * 1024


def _cparams(n_axes):
    return pltpu.CompilerParams(
        dimension_semantics=("arbitrary",) * n_axes, vmem_limit_bytes=VMEM_LIMIT)


def _sigmoid(x):
    return 1.0 / (1.0 + jnp.exp(-x))


def _silu(x):
    return x * _sigmoid(x)


def _dot(a, b):
    return jnp.dot(a.astype(BF16), b.astype(BF16), preferred_element_type=F32)


def _dot_nt(a, b):
    return lax.dot_general(a.astype(BF16), b.astype(BF16), (((1,), (1,)), ((), ())),
                           preferred_element_type=F32)


def _dot_tn(a, b):
    return lax.dot_general(a.astype(BF16), b.astype(BF16), (((0,), (0,)), ((), ())),
                           preferred_element_type=F32)


def _mm(a, b):
    return _dot(a, b)


def _dot_split3(m_bf16, x):
    h1 = x.astype(BF16)
    r1 = x - h1.astype(F32)
    h2 = r1.astype(BF16)
    h3 = (r1 - h2.astype(F32)).astype(BF16)
    d = lambda h: jnp.dot(m_bf16, h, preferred_element_type=F32)
    return d(h1) + d(h2) + d(h3)


def _rms(x, g):
    return x * lax.rsqrt(jnp.mean(x * x, axis=-1, keepdims=True) + EPS) * g


def _adaln_kernel(c_ref, w_ref, b_ref, o_ref):
    cond = _silu(c_ref[...])
    o_ref[0] = jnp.dot(cond, w_ref[0], precision=lax.Precision.HIGHEST,
                       preferred_element_type=F32) + b_ref[0]


def _adaln(c_pad, ada_w, ada_b):
    depth, d, cols = ada_w.shape
    tn = 1536
    return pl.pallas_call(
        _adaln_kernel,
        grid=(depth, cols // tn),
        in_specs=[
            pl.BlockSpec((SUBLANES, d), lambda l, j: (0, 0)),
            pl.BlockSpec((1, d, tn), lambda l, j: (l, 0, j)),
            pl.BlockSpec((1, 1, tn), lambda l, j: (l, 0, j)),
        ],
        out_specs=pl.BlockSpec((1, SUBLANES, tn), lambda l, j: (l, 0, j)),
        out_shape=jax.ShapeDtypeStruct((depth, SUBLANES, cols), F32),
        compiler_params=_cparams(2),
        name="adaln",
    )(c_pad, ada_w, ada_b.reshape(depth, 1, cols))


def _proj_kernel(x_ref, g_ref, sc_ref, sh_ref, w_ref, ws_ref, o_ref, os_ref, h_ref):
    @pl.when(pl.program_id(1) == 0)
    def _():
        h = _rms(x_ref[...], g_ref[...]) * (1.0 + sc_ref[0]) + sh_ref[0]
        hb = h.astype(BF16)
        h_ref[...] = hb
        os_ref[...] = jnp.dot(hb, ws_ref[...], preferred_element_type=F32)

    o_ref[...] = jnp.dot(h_ref[...], w_ref[...], preferred_element_type=F32)


def _proj(x, g, sc, sh, w_main, w_small, seq):
    n, d = x.shape
    cols = w_main.shape[1]
    tm, tn = 1024, 1536
    tpb = seq // tm
    return pl.pallas_call(
        _proj_kernel,
        grid=(n // tm, cols // tn),
        in_specs=[
            pl.BlockSpec((tm, d), lambda i, j: (i, 0)),
            pl.BlockSpec((1, d), lambda i, j: (0, 0)),
            pl.BlockSpec((1, 1, d), lambda i, j: (i // tpb, 0, 0)),
            pl.BlockSpec((1, 1, d), lambda i, j: (i // tpb, 0, 0)),
            pl.BlockSpec((d, tn), lambda i, j: (0, j)),
            pl.BlockSpec((d, LANES), lambda i, j: (0, 0)),
        ],
        out_specs=[
            pl.BlockSpec((tm, tn), lambda i, j: (i, j)),
            pl.BlockSpec((tm, LANES), lambda i, j: (i, 0)),
        ],
        out_shape=[
            jax.ShapeDtypeStruct((n, cols), F32),
            jax.ShapeDtypeStruct((n, LANES), F32),
        ],
        scratch_shapes=[pltpu.VMEM((tm, d), BF16)],
        compiler_params=_cparams(2),
        name="in_proj",
    )(x, g, sc, sh, w_main, w_small)


def _hgrn2_kernel(q_ref, f_ref, i_ref, g_ref, lb_ref, ng_ref, tri_ref, y_ref, st_ref, *, n_chunks):
    @pl.when(pl.program_id(1) == 0)
    def _():
        st_ref[...] = jnp.zeros_like(st_ref)

    lb = lb_ref[...]
    ng = ng_ref[...]
    tri = tri_ref[...]
    sub = CHUNK // 4
    row = lax.broadcasted_iota(jnp.int32, (CHUNK, CHUNK), 0)
    col = lax.broadcasted_iota(jnp.int32, (CHUNK, CHUNK), 1)
    causal = col <= row

    probs = [(c, h) for c in range(n_chunks) for h in range(N_HEADS)]
    b_chunks, f_chunks = [], []
    for c in range(n_chunks):
        rows = slice(c * CHUNK, (c + 1) * CHUNK)
        fgate = lb + (1.0 - lb) * _sigmoid(f_ref[rows, :])
        f_chunks.append(fgate)
        b_chunks.append(_dot_split3(tri, jnp.log(fgate)))
    bs, kks, qqs, vvs = [], [], [], []
    for c, h in probs:
        rows = slice(c * CHUNK, (c + 1) * CHUNK)
        hs = slice(h * HEAD_DIM, (h + 1) * HEAD_DIM)
        bs.append(b_chunks[c][:, hs])
        kks.append(1.0 - f_chunks[c][:, hs])
        qqs.append(_silu(q_ref[rows, hs]))
        vvs.append(i_ref[rows, hs])
    atts = []
    for b, kk, qq in zip(bs, kks, qqs):
        att_rows = []
        for blk in range(4):
            r0 = blk * sub
            ref_b = b[r0:r0 + 1, :]
            qe = qq[r0:r0 + sub, :] * jnp.exp(b[r0:r0 + sub, :] - ref_b)
            ke = kk * jnp.exp(jnp.minimum(ref_b - b, EXP_CLAMP))
            att_rows.append(_dot_nt(qe, ke))
        atts.append(jnp.where(causal, jnp.concatenate(att_rows, axis=0), 0.0))
    incs = [_dot_tn(vv, kk * jnp.exp(b[CHUNK - 1:CHUNK, :] - b)) for b, kk, vv in zip(bs, kks, vvs)]
    o_intra = [_dot(att, vv) for att, vv in zip(atts, vvs)]
    states = []
    cur = [st_ref[h] for h in range(N_HEADS)]
    for i, (c, h) in enumerate(probs):
        states.append(cur[h])
        cur[h] = cur[h] * jnp.exp(bs[i][CHUNK - 1:CHUNK, :]) + incs[i]
    for h in range(N_HEADS):
        st_ref[h] = cur[h]
    o_inter = [_dot_nt(qq * jnp.exp(b), st) for qq, b, st in zip(qqs, bs, states)]
    for i, (c, h) in enumerate(probs):
        rows = slice(c * CHUNK, (c + 1) * CHUNK)
        hs = slice(h * HEAD_DIM, (h + 1) * HEAD_DIM)
        y = _rms(o_inter[i] + o_intra[i], ng) * _silu(g_ref[rows, hs])
        y_ref[rows, hs] = y.astype(y_ref.dtype)


def _hgrn2(proj, lb, norm_g, tri, batch, seq):
    t = 256
    spb = seq // t
    col = lambda k: (lambda b, s: (b * spb + s, k))
    const = lambda b, s: (0, 0)
    return pl.pallas_call(
        functools.partial(_hgrn2_kernel, n_chunks=t // CHUNK),
        grid=(batch, spb),
        in_specs=[
            pl.BlockSpec((t, BRANCH_W), col(0)),
            pl.BlockSpec((t, BRANCH_W), col(1)),
            pl.BlockSpec((t, BRANCH_W), col(2)),
            pl.BlockSpec((t, BRANCH_W), col(3)),
            pl.BlockSpec((1, BRANCH_W), const),
            pl.BlockSpec((1, HEAD_DIM), const),
            pl.BlockSpec((CHUNK, CHUNK), const),
        ],
        out_specs=pl.BlockSpec((t, BRANCH_W), lambda b, s: (b * spb + s, 0)),
        out_shape=jax.ShapeDtypeStruct((batch * seq, BRANCH_W), BF16),
        scratch_shapes=[pltpu.VMEM((N_HEADS, HEAD_DIM, HEAD_DIM), F32)],
        compiler_params=_cparams(2),
        name="hgrn2",
    )(proj, proj, proj, proj, lb, norm_g, tri)


def _softplus(x):
    return jnp.maximum(x, 0.0) + jnp.log(1.0 + jnp.exp(-jnp.abs(x)))


def _gdn_kernel(q_ref, k_ref, v_ref, g_ref, ba_ref, cw_ref, alog_ref, dtb_ref, ng_ref, tri_ref,
                y_ref, xb_ref, qn_ref, kn_ref, vn_ref, st_ref, *, t, n_chunks):
    first = pl.program_id(1) == 0

    @pl.when(first)
    def _():
        st_ref[...] = jnp.zeros_like(st_ref)
        xb_ref[:, 0:SUBLANES, :] = jnp.zeros((3, SUBLANES, BRANCH_W), F32)

    @pl.when(jnp.logical_not(first))
    def _():
        xb_ref[:, 0:SUBLANES, :] = xb_ref[:, t:t + SUBLANES, :]

    for j, (src, dst, scale) in enumerate(((q_ref, qn_ref, HEAD_DIM ** -0.5), (k_ref, kn_ref, 1.0),
                                           (v_ref, vn_ref, None))):
        xb_ref[j, SUBLANES:SUBLANES + t, :] = src[...]
        acc = jnp.zeros((t, BRANCH_W), F32)
        for tap in range(CONV_K):
            off = SUBLANES - (CONV_K - 1) + tap
            w = cw_ref[tap:tap + 1, j * BRANCH_W:(j + 1) * BRANCH_W]
            acc = acc + xb_ref[j, off:off + t, :] * w
        acc = _silu(acc)
        if scale is None:
            dst[...] = acc
        else:
            for h in range(N_HEADS):
                hs = slice(h * HEAD_DIM, (h + 1) * HEAD_DIM)
                xh = acc[:, hs]
                dst[:, hs] = xh * (lax.rsqrt(jnp.sum(xh * xh, axis=-1, keepdims=True) + EPS) * scale)

    tri = tri_ref[...]
    ng = ng_ref[...]
    row = lax.broadcasted_iota(jnp.int32, (CHUNK, CHUNK), 0)
    col = lax.broadcasted_iota(jnp.int32, (CHUNK, CHUNK), 1)
    causal = col <= row
    strict = col < row
    eye = (col == row).astype(F32)
    blk16 = (row // 16) == (col // 16)
    blk32 = (row // 32) == (col // 32)

    probs = [(c, h) for c in range(n_chunks) for h in range(N_HEADS)]
    qs, ks, vs, betas, bcols, decs = [], [], [], [], [], []
    for c in range(n_chunks):
        rows = slice(c * CHUNK, (c + 1) * CHUNK)
        ba = ba_ref[rows, :]
        beta_all = _sigmoid(ba)
        logdec = -jnp.exp(alog_ref[...]) * _softplus(ba + dtb_ref[...])
        b_all = _dot_split3(tri, logdec)
        b_pad = jnp.concatenate([b_all, jnp.zeros((LANES - CHUNK, LANES), F32)], axis=0)
        b_t = b_pad.T
        for h in range(N_HEADS):
            hs = slice(h * HEAD_DIM, (h + 1) * HEAD_DIM)
            lane = N_HEADS + h
            bcol = b_all[:, lane:lane + 1]
            brow = b_t[lane:lane + 1, 0:CHUNK]
            bcols.append(bcol)
            betas.append(beta_all[:, h:h + 1])
            decs.append(jnp.exp(jnp.where(causal, bcol - brow, NEG)))
            qs.append(qn_ref[rows, hs])
            ks.append(kn_ref[rows, hs])
            vs.append(vn_ref[rows, hs])
    kbs = [k * beta for k, beta in zip(ks, betas)]
    a_mats = [jnp.where(strict, _dot_nt(kb, k) * dec, 0.0) for kb, k, dec in zip(kbs, ks, decs)]
    aqks = [_dot_nt(q, k) * dec for q, k, dec in zip(qs, ks, decs)]
    a_diags = [jnp.where(blk16, a, 0.0) for a in a_mats]
    ps = [eye - a for a in a_diags]
    xs = [_mm(a, a) for a in a_diags]
    for step in range(3):
        pxs = [_mm(p, x) for p, x in zip(ps, xs)]
        if step < 2:
            xs = [_mm(x, x) for x in xs]
        ps = [p + px for p, px in zip(ps, pxs)]
    for low in ([jnp.where(blk32, a, 0.0) - d for a, d in zip(a_mats, a_diags)],
                [jnp.where(blk32, 0.0, a) for a in a_mats]):
        ms = [_mm(p, lo) for p, lo in zip(ps, low)]
        ps = [p - _mm(m, p) for p, m in zip(ps, ms)]
    rhss = [jnp.concatenate([v * beta, kb * jnp.exp(bcol)], axis=1)
            for v, beta, kb, bcol in zip(vs, betas, kbs, bcols)]
    uws = [rhs + _mm(jnp.where(strict, p, 0.0), rhs) for p, rhs in zip(ps, rhss)]
    wqs = [jnp.concatenate([uw[:, HEAD_DIM:], q * jnp.exp(bcol)], axis=0)
           for uw, q, bcol in zip(uws, qs, bcols)]
    kds = [k * jnp.exp(bcol[CHUNK - 1:CHUNK, :] - bcol) for k, bcol in zip(ks, bcols)]
    cur = [st_ref[h] for h in range(N_HEADS)]
    outs = [None] * len(probs)
    for c in range(n_chunks):
        idx = [c * N_HEADS + h for h in range(N_HEADS)]
        wss = [_dot(wqs[i], cur[h]) for h, i in enumerate(idx)]
        v_news = [uws[i][:, :HEAD_DIM] - ws[:CHUNK] for i, ws in zip(idx, wss)]
        o_new = [_dot(aqks[i], v_new) for i, v_new in zip(idx, v_news)]
        incs = [_dot_tn(kds[i], v_new) for i, v_new in zip(idx, v_news)]
        for h, i in enumerate(idx):
            outs[i] = wss[h][CHUNK:] + o_new[h]
            cur[h] = cur[h] * jnp.exp(bcols[i][CHUNK - 1:CHUNK, :]) + incs[h]
    for h in range(N_HEADS):
        st_ref[h] = cur[h]
    for i, (c, h) in enumerate(probs):
        rows = slice(c * CHUNK, (c + 1) * CHUNK)
        hs = slice(h * HEAD_DIM, (h + 1) * HEAD_DIM)
        y = _rms(outs[i], ng) * _silu(g_ref[rows, hs])
        y_ref[rows, hs] = y.astype(y_ref.dtype)


def _gdn(proj, ba, conv_w, alog_row, dtb_row, norm_g, tri, batch, seq):
    t = 256
    spb = seq // t
    col = lambda k: (lambda b, s: (b * spb + s, k))
    const = lambda b, s: (0, 0)
    return pl.pallas_call(
        functools.partial(_gdn_kernel, t=t, n_chunks=t // CHUNK),
        grid=(batch, spb),
        in_specs=[
            pl.BlockSpec((t, BRANCH_W), col(4)),
            pl.BlockSpec((t, BRANCH_W), col(5)),
            pl.BlockSpec((t, BRANCH_W), col(6)),
            pl.BlockSpec((t, BRANCH_W), col(7)),
            pl.BlockSpec((t, LANES), col(0)),
            pl.BlockSpec((CONV_K, 3 * BRANCH_W), const),
            pl.BlockSpec((1, LANES), const),
            pl.BlockSpec((1, LANES), const),
            pl.BlockSpec((1, HEAD_DIM), const),
            pl.BlockSpec((CHUNK, CHUNK), const),
        ],
        out_specs=pl.BlockSpec((t, BRANCH_W), lambda b, s: (b * spb + s, 0)),
        out_shape=jax.ShapeDtypeStruct((batch * seq, BRANCH_W), BF16),
        scratch_shapes=[
            pltpu.VMEM((3, t + 2 * SUBLANES, BRANCH_W), F32),
            pltpu.VMEM((t, BRANCH_W), F32),
            pltpu.VMEM((t, BRANCH_W), F32),
            pltpu.VMEM((t, BRANCH_W), F32),
            pltpu.VMEM((N_HEADS, HEAD_DIM, HEAD_DIM), F32),
        ],
        compiler_params=_cparams(2),
        name="gdn",
    )(proj, proj, proj, proj, ba, conv_w, alog_row, dtb_row, norm_g, tri)


def _merge_kernel(yh_ref, yg_ref, gh_ref, gg_ref, x_ref, gt_ref, wbh_ref, wbg_ref, wo_ref,
                  g2_ref, sc_ref, sh_ref, wq_ref, xo_ref, h2t_ref, q_ref):
    merged = (_sigmoid(gh_ref[...]) * jnp.dot(yh_ref[...], wbh_ref[...], preferred_element_type=F32)
              + _sigmoid(gg_ref[...]) * jnp.dot(yg_ref[...], wbg_ref[...], preferred_element_type=F32))
    y = jnp.dot(merged.astype(BF16), wo_ref[...], preferred_element_type=F32)
    xn = x_ref[...] + gt_ref[0] * y
    xo_ref[...] = xn
    h2 = _rms(xn, g2_ref[...]) * (1.0 + sc_ref[0]) + sh_ref[0]
    h2t_ref[...] = pltpu.bitcast(h2.T.astype(BF16), jnp.uint32)
    q_ref[...] = jnp.dot(h2.astype(BF16), wq_ref[...], preferred_element_type=F32).astype(q_ref.dtype)


def _merge(y_hg, y_gdn, proj, x, gt1, w_bh, w_bg, w_out, g2, sc2, sh2, wq, seq):
    n, d = x.shape
    tm = 512
    tpb = seq // tm
    qw = wq.shape[1]
    const = lambda i: (0, 0)
    per_b = lambda i: (i // tpb, 0, 0)
    gate_blk = 8 * BRANCH_W // D_MODEL
    return pl.pallas_call(
        _merge_kernel,
        grid=(n // tm,),
        in_specs=[
            pl.BlockSpec((tm, BRANCH_W), lambda i: (i, 0)),
            pl.BlockSpec((tm, BRANCH_W), lambda i: (i, 0)),
            pl.BlockSpec((tm, d), lambda i: (i, gate_blk)),
            pl.BlockSpec((tm, d), lambda i: (i, gate_blk + 1)),
            pl.BlockSpec((tm, d), lambda i: (i, 0)),
            pl.BlockSpec((1, 1, d), per_b),
            pl.BlockSpec((BRANCH_W, d), const),
            pl.BlockSpec((BRANCH_W, d), const),
            pl.BlockSpec((d, d), const),
            pl.BlockSpec((1, d), const),
            pl.BlockSpec((1, 1, d), per_b),
            pl.BlockSpec((1, 1, d), per_b),
            pl.BlockSpec((d, qw), const),
        ],
        out_specs=[
            pl.BlockSpec((tm, d), lambda i: (i, 0)),
            pl.BlockSpec((d // 2, tm), lambda i: (0, i)),
            pl.BlockSpec((tm, qw), lambda i: (i, 0)),
        ],
        out_shape=[
            jax.ShapeDtypeStruct((n, d), F32),
            jax.ShapeDtypeStruct((d // 2, n), jnp.uint32),
            jax.ShapeDtypeStruct((n, qw), BF16),
        ],
        compiler_params=_cparams(1),
        name="merge",
    )(y_hg, y_gdn, proj, proj, x, gt1, w_bh, w_bg, w_out, g2, sc2, sh2, wq)


def _top_ranked(s, k):
    rid = lax.broadcasted_iota(jnp.int32, (k, s.shape[1]), 0)
    vals = jnp.zeros((k, s.shape[1]), F32)
    for r in range(k):
        m = jnp.max(s, axis=0, keepdims=True)
        vals = jnp.where(rid == r, m, vals)
        s = jnp.where(s == m, -(RANK_BASE + r * RANK_STEP), s)
    rank = jnp.where(s < -0.5 * RANK_BASE, jnp.round((-s - RANK_BASE) * (1.0 / RANK_STEP)), float(k))
    return vals, rank


def _route_kernel(q_ref, sk_ref, r1_ref, e1_ref, cnt_ref, e0_ref):
    p = q_ref.shape[0]
    kk = PEER_TOPK
    rid = lax.broadcasted_iota(jnp.int32, (SUBLANES, p), 0)
    for h in range(PEER_HEADS):
        q0 = q_ref[:, (2 * h) * PEER_HALF:(2 * h + 1) * PEER_HALF]
        q1 = q_ref[:, (2 * h + 1) * PEER_HALF:(2 * h + 2) * PEER_HALF]
        s0 = _dot_nt(sk_ref[h, 0], q0)
        s1 = _dot_nt(sk_ref[h, 1], q1)
        v0, rank0 = _top_ranked(s0, kk)
        v1, rank1 = _top_ranked(s1, kk)
        lo, hi = v1[0:SUBLANES, :], v1[SUBLANES:kk, :]
        pieces = [v0[0:1, :] + lo, v0[0:1, :] + hi, v0[1:2, :] + lo]
        for r0 in range(2, SUBLANES):
            pieces.append(jnp.where(rid < kk // (r0 + 1), v0[r0:r0 + 1, :] + lo, NEG))
        pieces.append(v0[SUBLANES:kk, :] + v1[0:1, :])
        work = list(pieces)
        tops = []
        for _ in range(kk):
            m = work[0]
            for a in work[1:]:
                m = jnp.maximum(m, a)
            m = jnp.max(m, axis=0, keepdims=True)
            tops.append(m)
            work = [jnp.where(a == m, NEG, a) for a in work]
        tau = tops[kk - 1]
        zsum = jnp.zeros_like(tau)
        for tk in tops:
            zsum = zsum + jnp.exp(tk - tops[0])
        sel = [jnp.where(pc >= tau, 1.0, 0.0) for pc in pieces]
        n_sel = [jnp.sum(sel[0] + sel[1], axis=0, keepdims=True)]
        n_sel += [jnp.sum(sel[r0 + 1], axis=0, keepdims=True) for r0 in range(1, SUBLANES)]
        n_high = jnp.sum(sel[SUBLANES + 1], axis=0, keepdims=True)
        high = (rank0 >= float(SUBLANES)) & (rank0 < float(SUBLANES) + n_high)
        cnt = jnp.where(high, 1.0, 0.0)
        for r0 in range(SUBLANES):
            cnt = jnp.where(rank0 == float(r0), n_sel[r0], cnt)
        r1_ref[h] = pltpu.bitcast(rank1.astype(BF16), jnp.uint32)
        e1_ref[h] = pltpu.bitcast(jnp.exp(s1 - v1[0:1, :]).astype(BF16), jnp.uint32)
        cnt_ref[h] = cnt
        e0_ref[h] = jnp.exp(s0 - v0[0:1, :]) * (0.5 / zsum)


def _route(q, subkeys_bf16):
    n, qw = q.shape
    p = LANES
    shp = jax.ShapeDtypeStruct((PEER_HEADS, PEER_NKEYS, n), F32)
    spec = pl.BlockSpec((PEER_HEADS, PEER_NKEYS, p), lambda i: (0, 0, i))
    shp_packed = jax.ShapeDtypeStruct((PEER_HEADS, PEER_NKEYS // 2, n), jnp.uint32)
    spec_packed = pl.BlockSpec((PEER_HEADS, PEER_NKEYS // 2, p), lambda i: (0, 0, i))
    return pl.pallas_call(
        _route_kernel,
        grid=(n // p,),
        in_specs=[
            pl.BlockSpec((p, qw), lambda i: (i, 0)),
            pl.BlockSpec((PEER_HEADS, 2, PEER_NKEYS, PEER_HALF), lambda i: (0, 0, 0, 0)),
        ],
        out_specs=[spec_packed, spec_packed, spec, spec],
        out_shape=[shp_packed, shp_packed, shp, shp],
        compiler_params=_cparams(1),
        name="peer_route",
    )(q, subkeys_bf16)


def _experts_kernel(ht_ref, u_ref, vt_ref, r1_ref, e1_ref, cnt_ref, e0_ref, x_ref, gt_ref, o_ref,
                    acc_ref, zt_ref, act_ref, *, ib, parts):
    step = pl.program_id(1)

    @pl.when(step == 0)
    def _():
        acc_ref[...] = jnp.zeros_like(acc_ref)

    hbt = pltpu.bitcast(ht_ref[...], BF16)
    p = hbt.shape[1]
    per_part = ib // parts
    width = per_part * PEER_NKEYS
    for part in range(parts):
        rows = slice(part * width, (part + 1) * width)
        u_part = pltpu.bitcast(u_ref[part * width // 2:(part + 1) * width // 2, :], BF16)
        zt_ref[rows, :] = jnp.dot(u_part, hbt, preferred_element_type=F32)
    for part in range(parts):
        iis = range(part * per_part, (part + 1) * per_part)
        for lt in range(p // LANES):
            ls = slice(lt * LANES, (lt + 1) * LANES)
            gs = [jnp.zeros((PEER_NKEYS // BF16_ROWS, BF16_ROWS, LANES), BF16) for _ in iis]
            for h in range(PEER_HEADS):
                r1 = pltpu.bitcast(r1_ref[h, :, ls], BF16).reshape(gs[0].shape)
                e1 = pltpu.bitcast(e1_ref[h, :, ls], BF16).reshape(gs[0].shape)
                for k, ii in enumerate(iis):
                    cnt = jnp.broadcast_to(cnt_ref[h, ii:ii + 1, ls], (BF16_ROWS, LANES)).astype(BF16)[None]
                    e0 = jnp.broadcast_to(e0_ref[h, ii:ii + 1, ls], (BF16_ROWS, LANES)).astype(BF16)[None]
                    gs[k] = gs[k] + e0 * jnp.where(r1 < cnt, e1, jnp.zeros((), BF16))
            for k, ii in enumerate(iis):
                rows = slice(ii * PEER_NKEYS, (ii + 1) * PEER_NKEYS)
                z = zt_ref[rows, ls]
                zg = (z * (1.0 + lax.erf(z * (2.0 ** -0.5)))).astype(BF16)
                act_ref[rows, ls] = zg * gs[k].reshape(PEER_NKEYS, LANES)
        rows = slice(part * width, (part + 1) * width)
        acc_ref[...] += jnp.dot(pltpu.bitcast(vt_ref[:, rows], BF16), act_ref[rows, :],
                                preferred_element_type=F32)

    @pl.when(step == pl.num_programs(1) - 1)
    def _():
        o_ref[...] = x_ref[...] + gt_ref[0] * acc_ref[...].T


def _experts(h2t, u_packed, vt_packed, layer, r1, e1, cnt, e0, x, gt2, seq):
    n, d = x.shape
    p = 512
    ib = 8
    tpb = seq // p
    blk = ib * PEER_NKEYS
    stat = pl.BlockSpec((PEER_HEADS, PEER_NKEYS // 2, p), lambda i, j: (0, 0, i))
    per_i = pl.BlockSpec((PEER_HEADS, ib, p), lambda i, j: (0, j, i))
    return pl.pallas_call(
        functools.partial(_experts_kernel, ib=ib, parts=4),
        grid=(n // p, PEER_NKEYS // ib),
        in_specs=[
            pl.BlockSpec((d // 2, p), lambda i, j: (0, i)),
            pl.BlockSpec((None, blk // 2, d), lambda i, j: (layer, j, 0)),
            pl.BlockSpec((None, d // 2, blk), lambda i, j: (layer, 0, j)),
            stat, stat, per_i, per_i,
            pl.BlockSpec((p, d), lambda i, j: (i, 0)),
            pl.BlockSpec((1, 1, d), lambda i, j: (i // tpb, 0, 0)),
        ],
        out_specs=pl.BlockSpec((p, d), lambda i, j: (i, 0)),
        out_shape=jax.ShapeDtypeStruct((n, d), F32),
        scratch_shapes=[pltpu.VMEM((d, p), F32), pltpu.VMEM((blk, p), F32), pltpu.VMEM((blk, p), BF16)],
        compiler_params=_cparams(2),
        name="peer_experts",
    )(h2t, u_packed, vt_packed, r1, e1, cnt, e0, x, gt2)


def _pack_tables_kernel(u_ref, v_ref, uo_ref, vo_ref):
    uo_ref[...] = pltpu.bitcast(u_ref[...].astype(BF16), jnp.uint32)
    vo_ref[...] = pltpu.bitcast(v_ref[...].T.astype(BF16), jnp.uint32)


def _pack_tables(peer_u, peer_v):
    depth, experts, d = peer_u.shape
    te = 1024
    return pl.pallas_call(
        _pack_tables_kernel,
        grid=(depth, experts // te),
        in_specs=[
            pl.BlockSpec((None, te, d), lambda l, j: (l, j, 0)),
            pl.BlockSpec((None, te, d), lambda l, j: (l, j, 0)),
        ],
        out_specs=[
            pl.BlockSpec((None, te // 2, d), lambda l, j: (l, j, 0)),
            pl.BlockSpec((None, d // 2, te), lambda l, j: (l, 0, j)),
        ],
        out_shape=[
            jax.ShapeDtypeStruct((depth, experts // 2, d), jnp.uint32),
            jax.ShapeDtypeStruct((depth, d // 2, experts), jnp.uint32),
        ],
        compiler_params=_cparams(2),
        name="pack_tables",
    )(peer_u, peer_v)


def _final_kernel(x_ref, g_ref, o_ref):
    o_ref[...] = _rms(x_ref[...], g_ref[...])


def _final_norm(x, g):
    n, d = x.shape
    tm = 1024
    return pl.pallas_call(
        _final_kernel,
        grid=(n // tm,),
        in_specs=[pl.BlockSpec((tm, d), lambda i: (i, 0)), pl.BlockSpec((1, d), lambda i: (0, 0))],
        out_specs=pl.BlockSpec((tm, d), lambda i: (i, 0)),
        out_shape=jax.ShapeDtypeStruct((n, d), F32),
        compiler_params=_cparams(1),
        name="final_norm",
    )(x, g)


def _lane_row(vals_lo, vals_hi):
    row = jnp.zeros((1, LANES), F32)
    row = row.at[0, 0:N_HEADS].set(vals_lo)
    return row.at[0, N_HEADS:2 * N_HEADS].set(vals_hi)


def kernel(x, c, ada_w, ada_b, norm1_g, norm2_g, final_g, w_in, hg_lb_logits, hg_norm_g, gdn_conv_w,
           gdn_a_log, gdn_dt_bias, gdn_norm_g, w_branch_hg, w_branch_gdn, w_out, peer_wq, peer_subkeys,
           peer_u, peer_v):
    batch, seq, d = x.shape
    depth = ada_w.shape[0]
    n = batch * seq

    sm = jax.nn.softmax(hg_lb_logits.astype(F32), axis=0)
    lower_bounds = jnp.cumsum(sm, axis=0) - sm[0:1]

    c_pad = jnp.zeros((SUBLANES, d), F32).at[:batch].set(c)
    mod = _adaln(c_pad, ada_w, ada_b)[:, :batch, :]

    u_packed, vt_packed = _pack_tables(peer_u, peer_v)
    tri = jnp.tril(jnp.ones((CHUNK, CHUNK), F32)).astype(BF16)
    zeros4 = jnp.zeros((N_HEADS,), F32)

    xf = x.reshape(n, d)
    for l in range(depth):
        sh1, sc1, gt1, sh2, sc2, gt2 = [m.reshape(batch, 1, d) for m in jnp.split(mod[l], 6, axis=-1)]
        w_l = w_in[l]
        k0 = 8 * BRANCH_W
        w_main = jnp.concatenate([w_l[:, :k0], w_l[:, k0 + 2 * N_HEADS:]], axis=1).astype(BF16)
        w_small = jnp.zeros((d, LANES), F32).at[:, :2 * N_HEADS].set(w_l[:, k0:k0 + 2 * N_HEADS]).astype(BF16)
        proj, ba = _proj(xf, norm1_g[l].reshape(1, d), sc1, sh1, w_main, w_small, seq)

        y_hg = _hgrn2(proj, lower_bounds[l].reshape(1, BRANCH_W), hg_norm_g[l].reshape(1, HEAD_DIM), tri,
                      batch, seq)
        y_gdn = _gdn(proj, ba, gdn_conv_w[l].astype(F32), _lane_row(zeros4, gdn_a_log[l].astype(F32)),
                     _lane_row(zeros4, gdn_dt_bias[l].astype(F32)), gdn_norm_g[l].reshape(1, HEAD_DIM), tri,
                     batch, seq)

        xf, h2t, q = _merge(y_hg, y_gdn, proj, xf, gt1, w_branch_hg[l].astype(BF16),
                            w_branch_gdn[l].astype(BF16), w_out[l].astype(BF16),
                            norm2_g[l].reshape(1, d), sc2, sh2, peer_wq[l].astype(BF16), seq)

        r1, e1, cnt, e0 = _route(q, peer_subkeys[l].astype(BF16))
        xf = _experts(h2t, u_packed, vt_packed, l, r1, e1, cnt, e0, xf, gt2, seq)

    return _final_norm(xf, final_g.reshape(1, d)).reshape(batch, seq, d)
```

```python
import functools

import jax
import jax.numpy as jnp
from jax import lax
from jax.experimental import pallas as pl
from jax.experimental.pallas import tpu as pltpu

F32 = jnp.float32
BF16 = jnp.bfloat16

D_MODEL = 1024
CHUNK = 64
N_HEADS = 4
HEAD_DIM = 128
BRANCH_W = N_HEADS * HEAD_DIM
CONV_K = 4
PEER_HEADS = 8
PEER_NKEYS = 128
PEER_HALF = 128
PEER_TOPK = 16
EPS = 1e-6
NEG = -1e30
EXP_CLAMP = 60.0
RANK_BASE = 1e30
RANK_STEP = 1e26

LANES = 128
SUBLANES = 8
BF16_ROWS = 16
VMEM_LIMIT = 56 * 1024 * 1024


def _cparams(n_axes):
    return pltpu.CompilerParams(
        dimension_semantics=("arbitrary",) * n_axes, vmem_limit_bytes=VMEM_LIMIT)


def _sigmoid(x):
    return 1.0 / (1.0 + jnp.exp(-x))


def _silu(x):
    return x * _sigmoid(x)


def _dot(a, b):
    return jnp.dot(a.astype(BF16), b.astype(BF16), preferred_element_type=F32)


def _dot_nt(a, b):
    return lax.dot_general(a.astype(BF16), b.astype(BF16), (((1,), (1,)), ((), ())),
                           preferred_element_type=F32)


def _dot_tn(a, b):
    return lax.dot_general(a.astype(BF16), b.astype(BF16), (((0,), (0,)), ((), ())),
                           preferred_element_type=F32)


def _mm(a, b):
    return _dot(a, b)


def _dot_split3(m_bf16, x):
    h1 = x.astype(BF16)
    r1 = x - h1.astype(F32)
    h2 = r1.astype(BF16)
    h3 = (r1 - h2.astype(F32)).astype(BF16)
    d = lambda h: jnp.dot(m_bf16, h, preferred_element_type=F32)
    return d(h1) + d(h2) + d(h3)


def _rms(x, g):
    return x * lax.rsqrt(jnp.mean(x * x, axis=-1, keepdims=True) + EPS) * g


def _adaln_kernel(c_ref, w_ref, b_ref, o_ref):
    cond = _silu(c_ref[...])
    o_ref[0] = jnp.dot(cond, w_ref[0], precision=lax.Precision.HIGHEST,
                       preferred_element_type=F32) + b_ref[0]


def _adaln(c_pad, ada_w, ada_b):
    depth, d, cols = ada_w.shape
    tn = 1536
    return pl.pallas_call(
        _adaln_kernel,
        grid=(depth, cols // tn),
        in_specs=[
            pl.BlockSpec((SUBLANES, d), lambda l, j: (0, 0)),
            pl.BlockSpec((1, d, tn), lambda l, j: (l, 0, j)),
            pl.BlockSpec((1, 1, tn), lambda l, j: (l, 0, j)),
        ],
        out_specs=pl.BlockSpec((1, SUBLANES, tn), lambda l, j: (l, 0, j)),
        out_shape=jax.ShapeDtypeStruct((depth, SUBLANES, cols), F32),
        compiler_params=_cparams(2),
        name="adaln",
    )(c_pad, ada_w, ada_b.reshape(depth, 1, cols))


def _proj_kernel(x_ref, g_ref, sc_ref, sh_ref, w_ref, ws_ref, o_ref, os_ref, h_ref):
    @pl.when(pl.program_id(1) == 0)
    def _():
        h = _rms(x_ref[...], g_ref[...]) * (1.0 + sc_ref[0]) + sh_ref[0]
        hb = h.astype(BF16)
        h_ref[...] = hb
        os_ref[...] = jnp.dot(hb, ws_ref[...], preferred_element_type=F32)

    o_ref[...] = jnp.dot(h_ref[...], w_ref[...], preferred_element_type=F32)


def _proj(x, g, sc, sh, w_main, w_small, seq):
    n, d = x.shape
    cols = w_main.shape[1]
    tm, tn = 1024, 1536
    tpb = seq // tm
    return pl.pallas_call(
        _proj_kernel,
        grid=(n // tm, cols // tn),
        in_specs=[
            pl.BlockSpec((tm, d), lambda i, j: (i, 0)),
            pl.BlockSpec((1, d), lambda i, j: (0, 0)),
            pl.BlockSpec((1, 1, d), lambda i, j: (i // tpb, 0, 0)),
            pl.BlockSpec((1, 1, d), lambda i, j: (i // tpb, 0, 0)),
            pl.BlockSpec((d, tn), lambda i, j: (0, j)),
            pl.BlockSpec((d, LANES), lambda i, j: (0, 0)),
        ],
        out_specs=[
            pl.BlockSpec((tm, tn), lambda i, j: (i, j)),
            pl.BlockSpec((tm, LANES), lambda i, j: (i, 0)),
        ],
        out_shape=[
            jax.ShapeDtypeStruct((n, cols), F32),
            jax.ShapeDtypeStruct((n, LANES), F32),
        ],
        scratch_shapes=[pltpu.VMEM((tm, d), BF16)],
        compiler_params=_cparams(2),
        name="in_proj",
    )(x, g, sc, sh, w_main, w_small)


def _hgrn2_kernel(q_ref, f_ref, i_ref, g_ref, lb_ref, ng_ref, tri_ref, y_ref, st_ref, *, n_chunks):
    @pl.when(pl.program_id(1) == 0)
    def _():
        st_ref[...] = jnp.zeros_like(st_ref)

    lb = lb_ref[...]
    ng = ng_ref[...]
    tri = tri_ref[...]
    sub = CHUNK // 4
    row = lax.broadcasted_iota(jnp.int32, (CHUNK, CHUNK), 0)
    col = lax.broadcasted_iota(jnp.int32, (CHUNK, CHUNK), 1)
    causal = col <= row

    probs = [(c, h) for c in range(n_chunks) for h in range(N_HEADS)]
    b_chunks, f_chunks = [], []
    for c in range(n_chunks):
        rows = slice(c * CHUNK, (c + 1) * CHUNK)
        fgate = lb + (1.0 - lb) * _sigmoid(f_ref[rows, :])
        f_chunks.append(fgate)
        b_chunks.append(_dot_split3(tri, jnp.log(fgate)))
    bs, kks, qqs, vvs = [], [], [], []
    for c, h in probs:
        rows = slice(c * CHUNK, (c + 1) * CHUNK)
        hs = slice(h * HEAD_DIM, (h + 1) * HEAD_DIM)
        bs.append(b_chunks[c][:, hs])
        kks.append(1.0 - f_chunks[c][:, hs])
        qqs.append(_silu(q_ref[rows, hs]))
        vvs.append(i_ref[rows, hs])
    atts = []
    for b, kk, qq in zip(bs, kks, qqs):
        att_rows = []
        for blk in range(4):
            r0 = blk * sub
            ref_b = b[r0:r0 + 1, :]
            qe = qq[r0:r0 + sub, :] * jnp.exp(b[r0:r0 + sub, :] - ref_b)
            ke = kk * jnp.exp(jnp.minimum(ref_b - b, EXP_CLAMP))
            att_rows.append(_dot_nt(qe, ke))
        atts.append(jnp.where(causal, jnp.concatenate(att_rows, axis=0), 0.0))
    incs = [_dot_tn(vv, kk * jnp.exp(b[CHUNK - 1:CHUNK, :] - b)) for b, kk, vv in zip(bs, kks, vvs)]
    o_intra = [_dot(att, vv) for att, vv in zip(atts, vvs)]
    states = []
    cur = [st_ref[h] for h in range(N_HEADS)]
    for i, (c, h) in enumerate(probs):
        states.append(cur[h])
        cur[h] = cur[h] * jnp.exp(bs[i][CHUNK - 1:CHUNK, :]) + incs[i]
    for h in range(N_HEADS):
        st_ref[h] = cur[h]
    o_inter = [_dot_nt(qq * jnp.exp(b), st) for qq, b, st in zip(qqs, bs, states)]
    for i, (c, h) in enumerate(probs):
        rows = slice(c * CHUNK, (c + 1) * CHUNK)
        hs = slice(h * HEAD_DIM, (h + 1) * HEAD_DIM)
        y = _rms(o_inter[i] + o_intra[i], ng) * _silu(g_ref[rows, hs])
        y_ref[rows, hs] = y.astype(y_ref.dtype)


def _hgrn2(proj, lb, norm_g, tri, batch, seq):
    t = 256
    spb = seq // t
    col = lambda k: (lambda b, s: (b * spb + s, k))
    const = lambda b, s: (0, 0)
    return pl.pallas_call(
        functools.partial(_hgrn2_kernel, n_chunks=t // CHUNK),
        grid=(batch, spb),
        in_specs=[
            pl.BlockSpec((t, BRANCH_W), col(0)),
            pl.BlockSpec((t, BRANCH_W), col(1)),
            pl.BlockSpec((t, BRANCH_W), col(2)),
            pl.BlockSpec((t, BRANCH_W), col(3)),
            pl.BlockSpec((1, BRANCH_W), const),
            pl.BlockSpec((1, HEAD_DIM), const),
            pl.BlockSpec((CHUNK, CHUNK), const),
        ],
        out_specs=pl.BlockSpec((t, BRANCH_W), lambda b, s: (b * spb + s, 0)),
        out_shape=jax.ShapeDtypeStruct((batch * seq, BRANCH_W), BF16),
        scratch_shapes=[pltpu.VMEM((N_HEADS, HEAD_DIM, HEAD_DIM), F32)],
        compiler_params=_cparams(2),
        name="hgrn2",
    )(proj, proj, proj, proj, lb, norm_g, tri)


def _softplus(x):
    return jnp.maximum(x, 0.0) + jnp.log(1.0 + jnp.exp(-jnp.abs(x)))


def _gdn_kernel(q_ref, k_ref, v_ref, g_ref, ba_ref, cw_ref, alog_ref, dtb_ref, ng_ref, tri_ref,
                y_ref, xb_ref, qn_ref, kn_ref, vn_ref, st_ref, *, t, n_chunks):
    first = pl.program_id(1) == 0

    @pl.when(first)
    def _():
        st_ref[...] = jnp.zeros_like(st_ref)
        xb_ref[:, 0:SUBLANES, :] = jnp.zeros((3, SUBLANES, BRANCH_W), F32)

    @pl.when(jnp.logical_not(first))
    def _():
        xb_ref[:, 0:SUBLANES, :] = xb_ref[:, t:t + SUBLANES, :]

    for j, (src, dst, scale) in enumerate(((q_ref, qn_ref, HEAD_DIM ** -0.5), (k_ref, kn_ref, 1.0),
                                           (v_ref, vn_ref, None))):
        xb_ref[j, SUBLANES:SUBLANES + t, :] = src[...]
        acc = jnp.zeros((t, BRANCH_W), F32)
        for tap in range(CONV_K):
            off = SUBLANES - (CONV_K - 1) + tap
            w = cw_ref[tap:tap + 1, j * BRANCH_W:(j + 1) * BRANCH_W]
            acc = acc + xb_ref[j, off:off + t, :] * w
        acc = _silu(acc)
        if scale is None:
            dst[...] = acc
        else:
            for h in range(N_HEADS):
                hs = slice(h * HEAD_DIM, (h + 1) * HEAD_DIM)
                xh = acc[:, hs]
                dst[:, hs] = xh * (lax.rsqrt(jnp.sum(xh * xh, axis=-1, keepdims=True) + EPS) * scale)

    tri = tri_ref[...]
    ng = ng_ref[...]
    row = lax.broadcasted_iota(jnp.int32, (CHUNK, CHUNK), 0)
    col = lax.broadcasted_iota(jnp.int32, (CHUNK, CHUNK), 1)
    causal = col <= row
    strict = col < row
    eye = (col == row).astype(F32)
    blk16 = (row // 16) == (col // 16)
    blk32 = (row // 32) == (col // 32)

    probs = [(c, h) for c in range(n_chunks) for h in range(N_HEADS)]
    qs, ks, vs, betas, bcols, decs = [], [], [], [], [], []
    for c in range(n_chunks):
        rows = slice(c * CHUNK, (c + 1) * CHUNK)
        ba = ba_ref[rows, :]
        beta_all = _sigmoid(ba)
        logdec = -jnp.exp(alog_ref[...]) * _softplus(ba + dtb_ref[...])
        b_all = _dot_split3(tri, logdec)
        b_pad = jnp.concatenate([b_all, jnp.zeros((LANES - CHUNK, LANES), F32)], axis=0)
        b_t = b_pad.T
        for h in range(N_HEADS):
            hs = slice(h * HEAD_DIM, (h + 1) * HEAD_DIM)
            lane = N_HEADS + h
            bcol = b_all[:, lane:lane + 1]
            brow = b_t[lane:lane + 1, 0:CHUNK]
            bcols.append(bcol)
            betas.append(beta_all[:, h:h + 1])
            decs.append(jnp.exp(jnp.where(causal, bcol - brow, NEG)))
            qs.append(qn_ref[rows, hs])
            ks.append(kn_ref[rows, hs])
            vs.append(vn_ref[rows, hs])
    kbs = [k * beta for k, beta in zip(ks, betas)]
    a_mats = [jnp.where(strict, _dot_nt(kb, k) * dec, 0.0) for kb, k, dec in zip(kbs, ks, decs)]
    aqks = [_dot_nt(q, k) * dec for q, k, dec in zip(qs, ks, decs)]
    a_diags = [jnp.where(blk16, a, 0.0) for a in a_mats]
    ps = [eye - a for a in a_diags]
    xs = [_mm(a, a) for a in a_diags]
    for step in range(3):
        pxs = [_mm(p, x) for p, x in zip(ps, xs)]
        if step < 2:
            xs = [_mm(x, x) for x in xs]
        ps = [p + px for p, px in zip(ps, pxs)]
    for low in ([jnp.where(blk32, a, 0.0) - d for a, d in zip(a_mats, a_diags)],
                [jnp.where(blk32, 0.0, a) for a in a_mats]):
        ms = [_mm(p, lo) for p, lo in zip(ps, low)]
        ps = [p - _mm(m, p) for p, m in zip(ps, ms)]
    rhss = [jnp.concatenate([v * beta, kb * jnp.exp(bcol)], axis=1)
            for v, beta, kb, bcol in zip(vs, betas, kbs, bcols)]
    uws = [rhs + _mm(jnp.where(strict, p, 0.0), rhs) for p, rhs in zip(ps, rhss)]
    wqs = [jnp.concatenate([uw[:, HEAD_DIM:], q * jnp.exp(bcol)], axis=0)
           for uw, q, bcol in zip(uws, qs, bcols)]
    kds = [k * jnp.exp(bcol[CHUNK - 1:CHUNK, :] - bcol) for k, bcol in zip(ks, bcols)]
    cur = [st_ref[h] for h in range(N_HEADS)]
    outs = [None] * len(probs)
    for c in range(n_chunks):
        idx = [c * N_HEADS + h for h in range(N_HEADS)]
        wss = [_dot(wqs[i], cur[h]) for h, i in enumerate(idx)]
        v_news = [uws[i][:, :HEAD_DIM] - ws[:CHUNK] for i, ws in zip(idx, wss)]
        o_new = [_dot(aqks[i], v_new) for i, v_new in zip(idx, v_news)]
        incs = [_dot_tn(kds[i], v_new) for i, v_new in zip(idx, v_news)]
        for h, i in enumerate(idx):
            outs[i] = wss[h][CHUNK:] + o_new[h]
            cur[h] = cur[h] * jnp.exp(bcols[i][CHUNK - 1:CHUNK, :]) + incs[h]
    for h in range(N_HEADS):
        st_ref[h] = cur[h]
    for i, (c, h) in enumerate(probs):
        rows = slice(c * CHUNK, (c + 1) * CHUNK)
        hs = slice(h * HEAD_DIM, (h + 1) * HEAD_DIM)
        y = _rms(outs[i], ng) * _silu(g_ref[rows, hs])
        y_ref[rows, hs] = y.astype(y_ref.dtype)


def _gdn(proj, ba, conv_w, alog_row, dtb_row, norm_g, tri, batch, seq):
    t = 256
    spb = seq // t
    col = lambda k: (lambda b, s: (b * spb + s, k))
    const = lambda b, s: (0, 0)
    return pl.pallas_call(
        functools.partial(_gdn_kernel, t=t, n_chunks=t // CHUNK),
        grid=(batch, spb),
        in_specs=[
            pl.BlockSpec((t, BRANCH_W), col(4)),
            pl.BlockSpec((t, BRANCH_W), col(5)),
            pl.BlockSpec((t, BRANCH_W), col(6)),
            pl.BlockSpec((t, BRANCH_W), col(7)),
            pl.BlockSpec((t, LANES), col(0)),
            pl.BlockSpec((CONV_K, 3 * BRANCH_W), const),
            pl.BlockSpec((1, LANES), const),
            pl.BlockSpec((1, LANES), const),
            pl.BlockSpec((1, HEAD_DIM), const),
            pl.BlockSpec((CHUNK, CHUNK), const),
        ],
        out_specs=pl.BlockSpec((t, BRANCH_W), lambda b, s: (b * spb + s, 0)),
        out_shape=jax.ShapeDtypeStruct((batch * seq, BRANCH_W), BF16),
        scratch_shapes=[
            pltpu.VMEM((3, t + 2 * SUBLANES, BRANCH_W), F32),
            pltpu.VMEM((t, BRANCH_W), F32),
            pltpu.VMEM((t, BRANCH_W), F32),
            pltpu.VMEM((t, BRANCH_W), F32),
            pltpu.VMEM((N_HEADS, HEAD_DIM, HEAD_DIM), F32),
        ],
        compiler_params=_cparams(2),
        name="gdn",
    )(proj, proj, proj, proj, ba, conv_w, alog_row, dtb_row, norm_g, tri)


def _merge_kernel(yh_ref, yg_ref, gh_ref, gg_ref, x_ref, gt_ref, wbh_ref, wbg_ref, wo_ref,
                  g2_ref, sc_ref, sh_ref, wq_ref, xo_ref, h2t_ref, q_ref):
    merged = (_sigmoid(gh_ref[...]) * jnp.dot(yh_ref[...], wbh_ref[...], preferred_element_type=F32)
              + _sigmoid(gg_ref[...]) * jnp.dot(yg_ref[...], wbg_ref[...], preferred_element_type=F32))
    y = jnp.dot(merged.astype(BF16), wo_ref[...], preferred_element_type=F32)
    xn = x_ref[...] + gt_ref[0] * y
    xo_ref[...] = xn
    h2 = _rms(xn, g2_ref[...]) * (1.0 + sc_ref[0]) + sh_ref[0]
    h2t_ref[...] = pltpu.bitcast(h2.T.astype(BF16), jnp.uint32)
    q_ref[...] = jnp.dot(h2.astype(BF16), wq_ref[...], preferred_element_type=F32).astype(q_ref.dtype)


def _merge(y_hg, y_gdn, proj, x, gt1, w_bh, w_bg, w_out, g2, sc2, sh2, wq, seq):
    n, d = x.shape
    tm = 512
    tpb = seq // tm
    qw = wq.shape[1]
    const = lambda i: (0, 0)
    per_b = lambda i: (i // tpb, 0, 0)
    gate_blk = 8 * BRANCH_W // D_MODEL
    return pl.pallas_call(
        _merge_kernel,
        grid=(n // tm,),
        in_specs=[
            pl.BlockSpec((tm, BRANCH_W), lambda i: (i, 0)),
            pl.BlockSpec((tm, BRANCH_W), lambda i: (i, 0)),
            pl.BlockSpec((tm, d), lambda i: (i, gate_blk)),
            pl.BlockSpec((tm, d), lambda i: (i, gate_blk + 1)),
            pl.BlockSpec((tm, d), lambda i: (i, 0)),
            pl.BlockSpec((1, 1, d), per_b),
            pl.BlockSpec((BRANCH_W, d), const),
            pl.BlockSpec((BRANCH_W, d), const),
            pl.BlockSpec((d, d), const),
            pl.BlockSpec((1, d), const),
            pl.BlockSpec((1, 1, d), per_b),
            pl.BlockSpec((1, 1, d), per_b),
            pl.BlockSpec((d, qw), const),
        ],
        out_specs=[
            pl.BlockSpec((tm, d), lambda i: (i, 0)),
            pl.BlockSpec((d // 2, tm), lambda i: (0, i)),
            pl.BlockSpec((tm, qw), lambda i: (i, 0)),
        ],
        out_shape=[
            jax.ShapeDtypeStruct((n, d), F32),
            jax.ShapeDtypeStruct((d // 2, n), jnp.uint32),
            jax.ShapeDtypeStruct((n, qw), BF16),
        ],
        compiler_params=_cparams(1),
        name="merge",
    )(y_hg, y_gdn, proj, proj, x, gt1, w_bh, w_bg, w_out, g2, sc2, sh2, wq)


def _top_ranked(s, k):
    rid = lax.broadcasted_iota(jnp.int32, (k, s.shape[1]), 0)
    vals = jnp.zeros((k, s.shape[1]), F32)
    for r in range(k):
        m = jnp.max(s, axis=0, keepdims=True)
        vals = jnp.where(rid == r, m, vals)
        s = jnp.where(s == m, -(RANK_BASE + r * RANK_STEP), s)
    rank = jnp.where(s < -0.5 * RANK_BASE, jnp.round((-s - RANK_BASE) * (1.0 / RANK_STEP)), float(k))
    return vals, rank


def _route_kernel(q_ref, sk_ref, r1_ref, e1_ref, cnt_ref, e0_ref):
    p = q_ref.shape[0]
    kk = PEER_TOPK
    rid = lax.broadcasted_iota(jnp.int32, (SUBLANES, p), 0)
    for h in range(PEER_HEADS):
        q0 = q_ref[:, (2 * h) * PEER_HALF:(2 * h + 1) * PEER_HALF]
        q1 = q_ref[:, (2 * h + 1) * PEER_HALF:(2 * h + 2) * PEER_HALF]
        s0 = _dot_nt(sk_ref[h, 0], q0)
        s1 = _dot_nt(sk_ref[h, 1], q1)
        v0, rank0 = _top_ranked(s0, kk)
        v1, rank1 = _top_ranked(s1, kk)
        lo, hi = v1[0:SUBLANES, :], v1[SUBLANES:kk, :]
        pieces = [v0[0:1, :] + lo, v0[0:1, :] + hi, v0[1:2, :] + lo]
        for r0 in range(2, SUBLANES):
            pieces.append(jnp.where(rid < kk // (r0 + 1), v0[r0:r0 + 1, :] + lo, NEG))
        pieces.append(v0[SUBLANES:kk, :] + v1[0:1, :])
        work = list(pieces)
        tops = []
        for _ in range(kk):
            m = work[0]
            for a in work[1:]:
                m = jnp.maximum(m, a)
            m = jnp.max(m, axis=0, keepdims=True)
            tops.append(m)
            work = [jnp.where(a == m, NEG, a) for a in work]
        tau = tops[kk - 1]
        zsum = jnp.zeros_like(tau)
        for tk in tops:
            zsum = zsum + jnp.exp(tk - tops[0])
        sel = [jnp.where(pc >= tau, 1.0, 0.0) for pc in pieces]
        n_sel = [jnp.sum(sel[0] + sel[1], axis=0, keepdims=True)]
        n_sel += [jnp.sum(sel[r0 + 1], axis=0, keepdims=True) for r0 in range(1, SUBLANES)]
        n_high = jnp.sum(sel[SUBLANES + 1], axis=0, keepdims=True)
        high = (rank0 >= float(SUBLANES)) & (rank0 < float(SUBLANES) + n_high)
        cnt = jnp.where(high, 1.0, 0.0)
        for r0 in range(SUBLANES):
            cnt = jnp.where(rank0 == float(r0), n_sel[r0], cnt)
        r1_ref[h] = pltpu.bitcast(rank1.astype(BF16), jnp.uint32)
        e1_ref[h] = pltpu.bitcast(jnp.exp(s1 - v1[0:1, :]).astype(BF16), jnp.uint32)
        cnt_ref[h] = cnt
        e0_ref[h] = jnp.exp(s0 - v0[0:1, :]) * (0.5 / zsum)


def _route(q, subkeys_bf16):
    n, qw = q.shape
    p = LANES
    shp = jax.ShapeDtypeStruct((PEER_HEADS, PEER_NKEYS, n), F32)
    spec = pl.BlockSpec((PEER_HEADS, PEER_NKEYS, p), lambda i: (0, 0, i))
    shp_packed = jax.ShapeDtypeStruct((PEER_HEADS, PEER_NKEYS // 2, n), jnp.uint32)
    spec_packed = pl.BlockSpec((PEER_HEADS, PEER_NKEYS // 2, p), lambda i: (0, 0, i))
    return pl.pallas_call(
        _route_kernel,
        grid=(n // p,),
        in_specs=[
            pl.BlockSpec((p, qw), lambda i: (i, 0)),
            pl.BlockSpec((PEER_HEADS, 2, PEER_NKEYS, PEER_HALF), lambda i: (0, 0, 0, 0)),
        ],
        out_specs=[spec_packed, spec_packed, spec, spec],
        out_shape=[shp_packed, shp_packed, shp, shp],
        compiler_params=_cparams(1),
        name="peer_route",
    )(q, subkeys_bf16)


def _experts_kernel(ht_ref, u_ref, vt_ref, r1_ref, e1_ref, cnt_ref, e0_ref, x_ref, gt_ref, o_ref,
                    acc_ref, zt_ref, act_ref, *, ib, parts):
    step = pl.program_id(1)

    @pl.when(step == 0)
    def _():
        acc_ref[...] = jnp.zeros_like(acc_ref)

    hbt = pltpu.bitcast(ht_ref[...], BF16)
    p = hbt.shape[1]
    per_part = ib // parts
    width = per_part * PEER_NKEYS
    for part in range(parts):
        rows = slice(part * width, (part + 1) * width)
        u_part = pltpu.bitcast(u_ref[part * width // 2:(part + 1) * width // 2, :], BF16)
        zt_ref[rows, :] = jnp.dot(u_part, hbt, preferred_element_type=F32)
    for part in range(parts):
        iis = range(part * per_part, (part + 1) * per_part)
        for lt in range(p // LANES):
            ls = slice(lt * LANES, (lt + 1) * LANES)
            gs = [jnp.zeros((PEER_NKEYS // BF16_ROWS, BF16_ROWS, LANES), BF16) for _ in iis]
            for h in range(PEER_HEADS):
                r1 = pltpu.bitcast(r1_ref[h, :, ls], BF16).reshape(gs[0].shape)
                e1 = pltpu.bitcast(e1_ref[h, :, ls], BF16).reshape(gs[0].shape)
                for k, ii in enumerate(iis):
                    cnt = jnp.broadcast_to(cnt_ref[h, ii:ii + 1, ls], (BF16_ROWS, LANES)).astype(BF16)[None]
                    e0 = jnp.broadcast_to(e0_ref[h, ii:ii + 1, ls], (BF16_ROWS, LANES)).astype(BF16)[None]
                    gs[k] = gs[k] + e0 * jnp.where(r1 < cnt, e1, jnp.zeros((), BF16))
            for k, ii in enumerate(iis):
                rows = slice(ii * PEER_NKEYS, (ii + 1) * PEER_NKEYS)
                z = zt_ref[rows, ls]
                zg = (z * (1.0 + lax.erf(z * (2.0 ** -0.5)))).astype(BF16)
                act_ref[rows, ls] = zg * gs[k].reshape(PEER_NKEYS, LANES)
        rows = slice(part * width, (part + 1) * width)
        acc_ref[...] += jnp.dot(pltpu.bitcast(vt_ref[:, rows], BF16), act_ref[rows, :],
                                preferred_element_type=F32)

    @pl.when(step == pl.num_programs(1) - 1)
    def _():
        o_ref[...] = x_ref[...] + gt_ref[0] * acc_ref[...].T


def _experts(h2t, u_packed, vt_packed, layer, r1, e1, cnt, e0, x, gt2, seq):
    n, d = x.shape
    p = min(1024, seq)
    ib = 8
    tpb = seq // p
    blk = ib * PEER_NKEYS
    stat = pl.BlockSpec((PEER_HEADS, PEER_NKEYS // 2, p), lambda i, j: (0, 0, i))
    per_i = pl.BlockSpec((PEER_HEADS, ib, p), lambda i, j: (0, j, i))
    return pl.pallas_call(
        functools.partial(_experts_kernel, ib=ib, parts=4),
        grid=(n // p, PEER_NKEYS // ib),
        in_specs=[
            pl.BlockSpec((d // 2, p), lambda i, j: (0, i)),
            pl.BlockSpec((None, blk // 2, d), lambda i, j: (layer, j, 0)),
            pl.BlockSpec((None, d // 2, blk), lambda i, j: (layer, 0, j)),
            stat, stat, per_i, per_i,
            pl.BlockSpec((p, d), lambda i, j: (i, 0)),
            pl.BlockSpec((1, 1, d), lambda i, j: (i // tpb, 0, 0)),
        ],
        out_specs=pl.BlockSpec((p, d), lambda i, j: (i, 0)),
        out_shape=jax.ShapeDtypeStruct((n, d), F32),
        scratch_shapes=[pltpu.VMEM((d, p), F32), pltpu.VMEM((blk, p), F32), pltpu.VMEM((blk, p), BF16)],
        compiler_params=_cparams(2),
        name="peer_experts",
    )(h2t, u_packed, vt_packed, r1, e1, cnt, e0, x, gt2)


def _pack_tables_kernel(u_ref, v_ref, uo_ref, vo_ref):
    uo_ref[...] = pltpu.bitcast(u_ref[...].astype(BF16), jnp.uint32)
    vo_ref[...] = pltpu.bitcast(v_ref[...].T.astype(BF16), jnp.uint32)


def _pack_tables(peer_u, peer_v):
    depth, experts, d = peer_u.shape
    te = 1024
    return pl.pallas_call(
        _pack_tables_kernel,
        grid=(depth, experts // te),
        in_specs=[
            pl.BlockSpec((None, te, d), lambda l, j: (l, j, 0)),
            pl.BlockSpec((None, te, d), lambda l, j: (l, j, 0)),
        ],
        out_specs=[
            pl.BlockSpec((None, te // 2, d), lambda l, j: (l, j, 0)),
            pl.BlockSpec((None, d // 2, te), lambda l, j: (l, 0, j)),
        ],
        out_shape=[
            jax.ShapeDtypeStruct((depth, experts // 2, d), jnp.uint32),
            jax.ShapeDtypeStruct((depth, d // 2, experts), jnp.uint32),
        ],
        compiler_params=_cparams(2),
        name="pack_tables",
    )(peer_u, peer_v)


def _final_kernel(x_ref, g_ref, o_ref):
    o_ref[...] = _rms(x_ref[...], g_ref[...])


def _final_norm(x, g):
    n, d = x.shape
    tm = 1024
    return pl.pallas_call(
        _final_kernel,
        grid=(n // tm,),
        in_specs=[pl.BlockSpec((tm, d), lambda i: (i, 0)), pl.BlockSpec((1, d), lambda i: (0, 0))],
        out_specs=pl.BlockSpec((tm, d), lambda i: (i, 0)),
        out_shape=jax.ShapeDtypeStruct((n, d), F32),
        compiler_params=_cparams(1),
        name="final_norm",
    )(x, g)


def _lane_row(vals_lo, vals_hi):
    row = jnp.zeros((1, LANES), F32)
    row = row.at[0, 0:N_HEADS].set(vals_lo)
    return row.at[0, N_HEADS:2 * N_HEADS].set(vals_hi)


def kernel(x, c, ada_w, ada_b, norm1_g, norm2_g, final_g, w_in, hg_lb_logits, hg_norm_g, gdn_conv_w,
           gdn_a_log, gdn_dt_bias, gdn_norm_g, w_branch_hg, w_branch_gdn, w_out, peer_wq, peer_subkeys,
           peer_u, peer_v):
    batch, seq, d = x.shape
    depth = ada_w.shape[0]
    n = batch * seq

    sm = jax.nn.softmax(hg_lb_logits.astype(F32), axis=0)
    lower_bounds = jnp.cumsum(sm, axis=0) - sm[0:1]

    c_pad = jnp.zeros((SUBLANES, d), F32).at[:batch].set(c)
    mod = _adaln(c_pad, ada_w, ada_b)[:, :batch, :]

    u_packed, vt_packed = _pack_tables(peer_u, peer_v)
    tri = jnp.tril(jnp.ones((CHUNK, CHUNK), F32)).astype(BF16)
    zeros4 = jnp.zeros((N_HEADS,), F32)

    xf = x.reshape(n, d)
    for l in range(depth):
        sh1, sc1, gt1, sh2, sc2, gt2 = [m.reshape(batch, 1, d) for m in jnp.split(mod[l], 6, axis=-1)]
        w_l = w_in[l]
        k0 = 8 * BRANCH_W
        w_main = jnp.concatenate([w_l[:, :k0], w_l[:, k0 + 2 * N_HEADS:]], axis=1).astype(BF16)
        w_small = jnp.zeros((d, LANES), F32).at[:, :2 * N_HEADS].set(w_l[:, k0:k0 + 2 * N_HEADS]).astype(BF16)
        proj, ba = _proj(xf, norm1_g[l].reshape(1, d), sc1, sh1, w_main, w_small, seq)

        y_hg = _hgrn2(proj, lower_bounds[l].reshape(1, BRANCH_W), hg_norm_g[l].reshape(1, HEAD_DIM), tri,
                      batch, seq)
        y_gdn = _gdn(proj, ba, gdn_conv_w[l].astype(F32), _lane_row(zeros4, gdn_a_log[l].astype(F32)),
                     _lane_row(zeros4, gdn_dt_bias[l].astype(F32)), gdn_norm_g[l].reshape(1, HEAD_DIM), tri,
                     batch, seq)

        xf, h2t, q = _merge(y_hg, y_gdn, proj, xf, gt1, w_branch_hg[l].astype(BF16),
                            w_branch_gdn[l].astype(BF16), w_out[l].astype(BF16),
                            norm2_g[l].reshape(1, d), sc2, sh2, peer_wq[l].astype(BF16), seq)

        r1, e1, cnt, e0 = _route(q, peer_subkeys[l].astype(BF16))
        xf = _experts(h2t, u_packed, vt_packed, l, r1, e1, cnt, e0, xf, gt2, seq)

    return _final_norm(xf, final_g.reshape(1, d)).reshape(batch, seq, d)
```

```python
import functools

import jax
import jax.numpy as jnp
from jax import lax
from jax.experimental import pallas as pl
from jax.experimental.pallas import tpu as pltpu

F32 = jnp.float32
BF16 = jnp.bfloat16

D_MODEL = 1024
CHUNK = 64
N_HEADS = 4
HEAD_DIM = 128
BRANCH_W = N_HEADS * HEAD_DIM
CONV_K = 4
PEER_HEADS = 8
PEER_NKEYS = 128
PEER_HALF = 128
PEER_TOPK = 16
EPS = 1e-6
NEG = -1e30
EXP_CLAMP = 60.0
RANK_BASE = 1e30
RANK_STEP = 1e26

LANES = 128
SUBLANES = 8
BF16_ROWS = 16
VMEM_LIMIT = 56 * 1024 * 1024


def _cparams(n_axes):
    return pltpu.CompilerParams(
        dimension_semantics=("arbitrary",) * n_axes, vmem_limit_bytes=VMEM_LIMIT)


def _sigmoid(x):
    return 1.0 / (1.0 + jnp.exp(-x))


def _silu(x):
    return x * _sigmoid(x)


def _dot(a, b):
    return jnp.dot(a.astype(BF16), b.astype(BF16), preferred_element_type=F32)


def _dot_nt(a, b):
    return lax.dot_general(a.astype(BF16), b.astype(BF16), (((1,), (1,)), ((), ())),
                           preferred_element_type=F32)


def _dot_tn(a, b):
    return lax.dot_general(a.astype(BF16), b.astype(BF16), (((0,), (0,)), ((), ())),
                           preferred_element_type=F32)


def _mm(a, b):
    return _dot(a, b)


def _dot_split3(m_bf16, x):
    h1 = x.astype(BF16)
    r1 = x - h1.astype(F32)
    h2 = r1.astype(BF16)
    h3 = (r1 - h2.astype(F32)).astype(BF16)
    d = lambda h: jnp.dot(m_bf16, h, preferred_element_type=F32)
    return d(h1) + d(h2) + d(h3)


def _rms(x, g):
    return x * lax.rsqrt(jnp.mean(x * x, axis=-1, keepdims=True) + EPS) * g


def _adaln_kernel(c_ref, w_ref, b_ref, o_ref):
    cond = _silu(c_ref[...])
    o_ref[0] = jnp.dot(cond, w_ref[0], precision=lax.Precision.HIGHEST,
                       preferred_element_type=F32) + b_ref[0]


def _adaln(c_pad, ada_w, ada_b):
    depth, d, cols = ada_w.shape
    tn = 1536
    return pl.pallas_call(
        _adaln_kernel,
        grid=(depth, cols // tn),
        in_specs=[
            pl.BlockSpec((SUBLANES, d), lambda l, j: (0, 0)),
            pl.BlockSpec((1, d, tn), lambda l, j: (l, 0, j)),
            pl.BlockSpec((1, 1, tn), lambda l, j: (l, 0, j)),
        ],
        out_specs=pl.BlockSpec((1, SUBLANES, tn), lambda l, j: (l, 0, j)),
        out_shape=jax.ShapeDtypeStruct((depth, SUBLANES, cols), F32),
        compiler_params=_cparams(2),
        name="adaln",
    )(c_pad, ada_w, ada_b.reshape(depth, 1, cols))


def _proj_kernel(x_ref, g_ref, sc_ref, sh_ref, w_ref, ws_ref, o_ref, os_ref, h_ref):
    @pl.when(pl.program_id(1) == 0)
    def _():
        h = _rms(x_ref[...], g_ref[...]) * (1.0 + sc_ref[0]) + sh_ref[0]
        hb = h.astype(BF16)
        h_ref[...] = hb
        os_ref[...] = jnp.dot(hb, ws_ref[...], preferred_element_type=F32)

    o_ref[...] = jnp.dot(h_ref[...], w_ref[...], preferred_element_type=F32)


def _proj(x, g, sc, sh, w_main, w_small, seq):
    n, d = x.shape
    cols = w_main.shape[1]
    tm, tn = 1024, 1536
    tpb = seq // tm
    return pl.pallas_call(
        _proj_kernel,
        grid=(n // tm, cols // tn),
        in_specs=[
            pl.BlockSpec((tm, d), lambda i, j: (i, 0)),
            pl.BlockSpec((1, d), lambda i, j: (0, 0)),
            pl.BlockSpec((1, 1, d), lambda i, j: (i // tpb, 0, 0)),
            pl.BlockSpec((1, 1, d), lambda i, j: (i // tpb, 0, 0)),
            pl.BlockSpec((d, tn), lambda i, j: (0, j)),
            pl.BlockSpec((d, LANES), lambda i, j: (0, 0)),
        ],
        out_specs=[
            pl.BlockSpec((tm, tn), lambda i, j: (i, j)),
            pl.BlockSpec((tm, LANES), lambda i, j: (i, 0)),
        ],
        out_shape=[
            jax.ShapeDtypeStruct((n, cols), F32),
            jax.ShapeDtypeStruct((n, LANES), F32),
        ],
        scratch_shapes=[pltpu.VMEM((tm, d), BF16)],
        compiler_params=_cparams(2),
        name="in_proj",
    )(x, g, sc, sh, w_main, w_small)


def _hgrn2_kernel(q_ref, f_ref, i_ref, g_ref, lb_ref, ng_ref, tri_ref, y_ref, st_ref, *, n_chunks):
    @pl.when(pl.program_id(1) == 0)
    def _():
        st_ref[...] = jnp.zeros_like(st_ref)

    lb = lb_ref[...]
    ng = ng_ref[...]
    tri = tri_ref[...]
    sub = CHUNK // 4
    row = lax.broadcasted_iota(jnp.int32, (CHUNK, CHUNK), 0)
    col = lax.broadcasted_iota(jnp.int32, (CHUNK, CHUNK), 1)
    causal = col <= row

    probs = [(c, h) for c in range(n_chunks) for h in range(N_HEADS)]
    b_chunks, f_chunks = [], []
    for c in range(n_chunks):
        rows = slice(c * CHUNK, (c + 1) * CHUNK)
        fgate = lb + (1.0 - lb) * _sigmoid(f_ref[rows, :])
        f_chunks.append(fgate)
        b_chunks.append(_dot_split3(tri, jnp.log(fgate)))
    bs, kks, qqs, vvs = [], [], [], []
    for c, h in probs:
        rows = slice(c * CHUNK, (c + 1) * CHUNK)
        hs = slice(h * HEAD_DIM, (h + 1) * HEAD_DIM)
        bs.append(b_chunks[c][:, hs])
        kks.append(1.0 - f_chunks[c][:, hs])
        qqs.append(_silu(q_ref[rows, hs]))
        vvs.append(i_ref[rows, hs])
    atts = []
    for b, kk, qq in zip(bs, kks, qqs):
        att_rows = []
        for blk in range(4):
            r0 = blk * sub
            ref_b = b[r0:r0 + 1, :]
            qe = qq[r0:r0 + sub, :] * jnp.exp(b[r0:r0 + sub, :] - ref_b)
            ke = kk * jnp.exp(jnp.minimum(ref_b - b, EXP_CLAMP))
            att_rows.append(_dot_nt(qe, ke))
        atts.append(jnp.where(causal, jnp.concatenate(att_rows, axis=0), 0.0))
    incs = [_dot_tn(vv, kk * jnp.exp(b[CHUNK - 1:CHUNK, :] - b)) for b, kk, vv in zip(bs, kks, vvs)]
    o_intra = [_dot(att, vv) for att, vv in zip(atts, vvs)]
    states = []
    cur = [st_ref[h] for h in range(N_HEADS)]
    for i, (c, h) in enumerate(probs):
        states.append(cur[h])
        cur[h] = cur[h] * jnp.exp(bs[i][CHUNK - 1:CHUNK, :]) + incs[i]
    for h in range(N_HEADS):
        st_ref[h] = cur[h]
    o_inter = [_dot_nt(qq * jnp.exp(b), st) for qq, b, st in zip(qqs, bs, states)]
    for i, (c, h) in enumerate(probs):
        rows = slice(c * CHUNK, (c + 1) * CHUNK)
        hs = slice(h * HEAD_DIM, (h + 1) * HEAD_DIM)
        y = _rms(o_inter[i] + o_intra[i], ng) * _silu(g_ref[rows, hs])
        y_ref[rows, hs] = y.astype(y_ref.dtype)


def _hgrn2(proj, lb, norm_g, tri, batch, seq):
    t = 256
    spb = seq // t
    col = lambda k: (lambda b, s: (b * spb + s, k))
    const = lambda b, s: (0, 0)
    return pl.pallas_call(
        functools.partial(_hgrn2_kernel, n_chunks=t // CHUNK),
        grid=(batch, spb),
        in_specs=[
            pl.BlockSpec((t, BRANCH_W), col(0)),
            pl.BlockSpec((t, BRANCH_W), col(1)),
            pl.BlockSpec((t, BRANCH_W), col(2)),
            pl.BlockSpec((t, BRANCH_W), col(3)),
            pl.BlockSpec((1, BRANCH_W), const),
            pl.BlockSpec((1, HEAD_DIM), const),
            pl.BlockSpec((CHUNK, CHUNK), const),
        ],
        out_specs=pl.BlockSpec((t, BRANCH_W), lambda b, s: (b * spb + s, 0)),
        out_shape=jax.ShapeDtypeStruct((batch * seq, BRANCH_W), BF16),
        scratch_shapes=[pltpu.VMEM((N_HEADS, HEAD_DIM, HEAD_DIM), F32)],
        compiler_params=_cparams(2),
        name="hgrn2",
    )(proj, proj, proj, proj, lb, norm_g, tri)


def _softplus(x):
    return jnp.maximum(x, 0.0) + jnp.log(1.0 + jnp.exp(-jnp.abs(x)))


def _gdn_kernel(q_ref, k_ref, v_ref, g_ref, ba_ref, cw_ref, alog_ref, dtb_ref, ng_ref, tri_ref,
                y_ref, xb_ref, qn_ref, kn_ref, vn_ref, st_ref, *, t, n_chunks):
    first = pl.program_id(1) == 0

    @pl.when(first)
    def _():
        st_ref[...] = jnp.zeros_like(st_ref)
        xb_ref[:, 0:SUBLANES, :] = jnp.zeros((3, SUBLANES, BRANCH_W), F32)

    @pl.when(jnp.logical_not(first))
    def _():
        xb_ref[:, 0:SUBLANES, :] = xb_ref[:, t:t + SUBLANES, :]

    for j, (src, dst, scale) in enumerate(((q_ref, qn_ref, HEAD_DIM ** -0.5), (k_ref, kn_ref, 1.0),
                                           (v_ref, vn_ref, None))):
        xb_ref[j, SUBLANES:SUBLANES + t, :] = src[...]
        acc = jnp.zeros((t, BRANCH_W), F32)
        for tap in range(CONV_K):
            off = SUBLANES - (CONV_K - 1) + tap
            w = cw_ref[tap:tap + 1, j * BRANCH_W:(j + 1) * BRANCH_W]
            acc = acc + xb_ref[j, off:off + t, :] * w
        acc = _silu(acc)
        if scale is None:
            dst[...] = acc
        else:
            for h in range(N_HEADS):
                hs = slice(h * HEAD_DIM, (h + 1) * HEAD_DIM)
                xh = acc[:, hs]
                dst[:, hs] = xh * (lax.rsqrt(jnp.sum(xh * xh, axis=-1, keepdims=True) + EPS) * scale)

    tri = tri_ref[...]
    ng = ng_ref[...]
    row = lax.broadcasted_iota(jnp.int32, (CHUNK, CHUNK), 0)
    col = lax.broadcasted_iota(jnp.int32, (CHUNK, CHUNK), 1)
    causal = col <= row
    strict = col < row
    eye = (col == row).astype(F32)
    blk16 = (row // 16) == (col // 16)
    blk32 = (row // 32) == (col // 32)

    probs = [(c, h) for c in range(n_chunks) for h in range(N_HEADS)]
    qs, ks, vs, betas, bcols, decs = [], [], [], [], [], []
    for c in range(n_chunks):
        rows = slice(c * CHUNK, (c + 1) * CHUNK)
        ba = ba_ref[rows, :]
        beta_all = _sigmoid(ba)
        logdec = -jnp.exp(alog_ref[...]) * _softplus(ba + dtb_ref[...])
        b_all = _dot_split3(tri, logdec)
        b_pad = jnp.concatenate([b_all, jnp.zeros((LANES - CHUNK, LANES), F32)], axis=0)
        b_t = b_pad.T
        for h in range(N_HEADS):
            hs = slice(h * HEAD_DIM, (h + 1) * HEAD_DIM)
            lane = N_HEADS + h
            bcol = b_all[:, lane:lane + 1]
            brow = b_t[lane:lane + 1, 0:CHUNK]
            bcols.append(bcol)
            betas.append(beta_all[:, h:h + 1])
            decs.append(jnp.exp(jnp.where(causal, bcol - brow, NEG)))
            qs.append(qn_ref[rows, hs])
            ks.append(kn_ref[rows, hs])
            vs.append(vn_ref[rows, hs])
    kbs = [k * beta for k, beta in zip(ks, betas)]
    a_mats = [jnp.where(strict, _dot_nt(kb, k) * dec, 0.0) for kb, k, dec in zip(kbs, ks, decs)]
    aqks = [_dot_nt(q, k) * dec for q, k, dec in zip(qs, ks, decs)]
    a_diags = [jnp.where(blk16, a, 0.0) for a in a_mats]
    ps = [eye - a for a in a_diags]
    xs = [_mm(a, a) for a in a_diags]
    for step in range(3):
        pxs = [_mm(p, x) for p, x in zip(ps, xs)]
        if step < 2:
            xs = [_mm(x, x) for x in xs]
        ps = [p + px for p, px in zip(ps, pxs)]
    for low in ([jnp.where(blk32, a, 0.0) - d for a, d in zip(a_mats, a_diags)],
                [jnp.where(blk32, 0.0, a) for a in a_mats]):
        ms = [_mm(p, lo) for p, lo in zip(ps, low)]
        ps = [p - _mm(m, p) for p, m in zip(ps, ms)]
    rhss = [jnp.concatenate([v * beta, kb * jnp.exp(bcol)], axis=1)
            for v, beta, kb, bcol in zip(vs, betas, kbs, bcols)]
    uws = [rhs + _mm(jnp.where(strict, p, 0.0), rhs) for p, rhs in zip(ps, rhss)]
    wqs = [jnp.concatenate([uw[:, HEAD_DIM:], q * jnp.exp(bcol)], axis=0)
           for uw, q, bcol in zip(uws, qs, bcols)]
    kds = [k * jnp.exp(bcol[CHUNK - 1:CHUNK, :] - bcol) for k, bcol in zip(ks, bcols)]
    cur = [st_ref[h] for h in range(N_HEADS)]
    outs = [None] * len(probs)
    for c in range(n_chunks):
        idx = [c * N_HEADS + h for h in range(N_HEADS)]
        wss = [_dot(wqs[i], cur[h]) for h, i in enumerate(idx)]
        v_news = [uws[i][:, :HEAD_DIM] - ws[:CHUNK] for i, ws in zip(idx, wss)]
        o_new = [_dot(aqks[i], v_new) for i, v_new in zip(idx, v_news)]
        incs = [_dot_tn(kds[i], v_new) for i, v_new in zip(idx, v_news)]
        for h, i in enumerate(idx):
            outs[i] = wss[h][CHUNK:] + o_new[h]
            cur[h] = cur[h] * jnp.exp(bcols[i][CHUNK - 1:CHUNK, :]) + incs[h]
    for h in range(N_HEADS):
        st_ref[h] = cur[h]
    for i, (c, h) in enumerate(probs):
        rows = slice(c * CHUNK, (c + 1) * CHUNK)
        hs = slice(h * HEAD_DIM, (h + 1) * HEAD_DIM)
        y = _rms(outs[i], ng) * _silu(g_ref[rows, hs])
        y_ref[rows, hs] = y.astype(y_ref.dtype)


def _gdn(proj, ba, conv_w, alog_row, dtb_row, norm_g, tri, batch, seq):
    t = 256
    spb = seq // t
    col = lambda k: (lambda b, s: (b * spb + s, k))
    const = lambda b, s: (0, 0)
    return pl.pallas_call(
        functools.partial(_gdn_kernel, t=t, n_chunks=t // CHUNK),
        grid=(batch, spb),
        in_specs=[
            pl.BlockSpec((t, BRANCH_W), col(4)),
            pl.BlockSpec((t, BRANCH_W), col(5)),
            pl.BlockSpec((t, BRANCH_W), col(6)),
            pl.BlockSpec((t, BRANCH_W), col(7)),
            pl.BlockSpec((t, LANES), col(0)),
            pl.BlockSpec((CONV_K, 3 * BRANCH_W), const),
            pl.BlockSpec((1, LANES), const),
            pl.BlockSpec((1, LANES), const),
            pl.BlockSpec((1, HEAD_DIM), const),
            pl.BlockSpec((CHUNK, CHUNK), const),
        ],
        out_specs=pl.BlockSpec((t, BRANCH_W), lambda b, s: (b * spb + s, 0)),
        out_shape=jax.ShapeDtypeStruct((batch * seq, BRANCH_W), BF16),
        scratch_shapes=[
            pltpu.VMEM((3, t + 2 * SUBLANES, BRANCH_W), F32),
            pltpu.VMEM((t, BRANCH_W), F32),
            pltpu.VMEM((t, BRANCH_W), F32),
            pltpu.VMEM((t, BRANCH_W), F32),
            pltpu.VMEM((N_HEADS, HEAD_DIM, HEAD_DIM), F32),
        ],
        compiler_params=_cparams(2),
        name="gdn",
    )(proj, proj, proj, proj, ba, conv_w, alog_row, dtb_row, norm_g, tri)


def _merge_kernel(yh_ref, yg_ref, gh_ref, gg_ref, x_ref, gt_ref, wbh_ref, wbg_ref, wo_ref,
                  g2_ref, sc_ref, sh_ref, wq_ref, xo_ref, h2t_ref, q_ref):
    merged = (_sigmoid(gh_ref[...]) * jnp.dot(yh_ref[...], wbh_ref[...], preferred_element_type=F32)
              + _sigmoid(gg_ref[...]) * jnp.dot(yg_ref[...], wbg_ref[...], preferred_element_type=F32))
    y = jnp.dot(merged.astype(BF16), wo_ref[...], preferred_element_type=F32)
    xn = x_ref[...] + gt_ref[0] * y
    xo_ref[...] = xn
    h2 = _rms(xn, g2_ref[...]) * (1.0 + sc_ref[0]) + sh_ref[0]
    h2t_ref[...] = pltpu.bitcast(h2.T.astype(BF16), jnp.uint32)
    q_ref[...] = jnp.dot(h2.astype(BF16), wq_ref[...], preferred_element_type=F32).astype(q_ref.dtype)


def _merge(y_hg, y_gdn, proj, x, gt1, w_bh, w_bg, w_out, g2, sc2, sh2, wq, seq):
    n, d = x.shape
    tm = 512
    tpb = seq // tm
    qw = wq.shape[1]
    const = lambda i: (0, 0)
    per_b = lambda i: (i // tpb, 0, 0)
    gate_blk = 8 * BRANCH_W // D_MODEL
    return pl.pallas_call(
        _merge_kernel,
        grid=(n // tm,),
        in_specs=[
            pl.BlockSpec((tm, BRANCH_W), lambda i: (i, 0)),
            pl.BlockSpec((tm, BRANCH_W), lambda i: (i, 0)),
            pl.BlockSpec((tm, d), lambda i: (i, gate_blk)),
            pl.BlockSpec((tm, d), lambda i: (i, gate_blk + 1)),
            pl.BlockSpec((tm, d), lambda i: (i, 0)),
            pl.BlockSpec((1, 1, d), per_b),
            pl.BlockSpec((BRANCH_W, d), const),
            pl.BlockSpec((BRANCH_W, d), const),
            pl.BlockSpec((d, d), const),
            pl.BlockSpec((1, d), const),
            pl.BlockSpec((1, 1, d), per_b),
            pl.BlockSpec((1, 1, d), per_b),
            pl.BlockSpec((d, qw), const),
        ],
        out_specs=[
            pl.BlockSpec((tm, d), lambda i: (i, 0)),
            pl.BlockSpec((d // 2, tm), lambda i: (0, i)),
            pl.BlockSpec((tm, qw), lambda i: (i, 0)),
        ],
        out_shape=[
            jax.ShapeDtypeStruct((n, d), F32),
            jax.ShapeDtypeStruct((d // 2, n), jnp.uint32),
            jax.ShapeDtypeStruct((n, qw), BF16),
        ],
        compiler_params=_cparams(1),
        name="merge",
    )(y_hg, y_gdn, proj, proj, x, gt1, w_bh, w_bg, w_out, g2, sc2, sh2, wq)


def _top_ranked(s, k):
    rid = lax.broadcasted_iota(jnp.int32, (k, s.shape[1]), 0)
    vals = jnp.zeros((k, s.shape[1]), F32)
    for r in range(k):
        m = jnp.max(s, axis=0, keepdims=True)
        vals = jnp.where(rid == r, m, vals)
        s = jnp.where(s == m, -(RANK_BASE + r * RANK_STEP), s)
    rank = jnp.where(s < -0.5 * RANK_BASE, jnp.round((-s - RANK_BASE) * (1.0 / RANK_STEP)), float(k))
    return vals, rank


def _route_kernel(q_ref, sk_ref, r1_ref, e1_ref, cnt_ref, e0_ref):
    p = q_ref.shape[0]
    kk = PEER_TOPK
    rid = lax.broadcasted_iota(jnp.int32, (SUBLANES, p), 0)
    for h in range(PEER_HEADS):
        q0 = q_ref[:, (2 * h) * PEER_HALF:(2 * h + 1) * PEER_HALF]
        q1 = q_ref[:, (2 * h + 1) * PEER_HALF:(2 * h + 2) * PEER_HALF]
        s0 = _dot_nt(sk_ref[h, 0], q0)
        s1 = _dot_nt(sk_ref[h, 1], q1)
        v0, rank0 = _top_ranked(s0, kk)
        v1, rank1 = _top_ranked(s1, kk)
        lo, hi = v1[0:SUBLANES, :], v1[SUBLANES:kk, :]
        pieces = [v0[0:1, :] + lo, v0[0:1, :] + hi, v0[1:2, :] + lo]
        for r0 in range(2, SUBLANES):
            pieces.append(jnp.where(rid < kk // (r0 + 1), v0[r0:r0 + 1, :] + lo, NEG))
        pieces.append(v0[SUBLANES:kk, :] + v1[0:1, :])
        work = list(pieces)
        tops = []
        for _ in range(kk):
            m = work[0]
            for a in work[1:]:
                m = jnp.maximum(m, a)
            m = jnp.max(m, axis=0, keepdims=True)
            tops.append(m)
            work = [jnp.where(a == m, NEG, a) for a in work]
        tau = tops[kk - 1]
        zsum = jnp.zeros_like(tau)
        for tk in tops:
            zsum = zsum + jnp.exp(tk - tops[0])
        sel = [jnp.where(pc >= tau, 1.0, 0.0) for pc in pieces]
        n_sel = [jnp.sum(sel[0] + sel[1], axis=0, keepdims=True)]
        n_sel += [jnp.sum(sel[r0 + 1], axis=0, keepdims=True) for r0 in range(1, SUBLANES)]
        n_high = jnp.sum(sel[SUBLANES + 1], axis=0, keepdims=True)
        high = (rank0 >= float(SUBLANES)) & (rank0 < float(SUBLANES) + n_high)
        cnt = jnp.where(high, 1.0, 0.0)
        for r0 in range(SUBLANES):
            cnt = jnp.where(rank0 == float(r0), n_sel[r0], cnt)
        r1_ref[h] = pltpu.bitcast(rank1.astype(BF16), jnp.uint32)
        e1_ref[h] = pltpu.bitcast(jnp.exp(s1 - v1[0:1, :]).astype(BF16), jnp.uint32)
        cnt_ref[h] = cnt
        e0_ref[h] = jnp.exp(s0 - v0[0:1, :]) * (0.5 / zsum)


def _route(q, subkeys_bf16):
    n, qw = q.shape
    p = LANES
    shp = jax.ShapeDtypeStruct((PEER_HEADS, PEER_NKEYS, n), F32)
    spec = pl.BlockSpec((PEER_HEADS, PEER_NKEYS, p), lambda i: (0, 0, i))
    shp_packed = jax.ShapeDtypeStruct((PEER_HEADS, PEER_NKEYS // 2, n), jnp.uint32)
    spec_packed = pl.BlockSpec((PEER_HEADS, PEER_NKEYS // 2, p), lambda i: (0, 0, i))
    return pl.pallas_call(
        _route_kernel,
        grid=(n // p,),
        in_specs=[
            pl.BlockSpec((p, qw), lambda i: (i, 0)),
            pl.BlockSpec((PEER_HEADS, 2, PEER_NKEYS, PEER_HALF), lambda i: (0, 0, 0, 0)),
        ],
        out_specs=[spec_packed, spec_packed, spec, spec],
        out_shape=[shp_packed, shp_packed, shp, shp],
        compiler_params=_cparams(1),
        name="peer_route",
    )(q, subkeys_bf16)


def _experts_kernel(ht_ref, u_ref, vt_ref, r1_ref, e1_ref, cnt_ref, e0_ref, x_ref, gt_ref, o_ref,
                    acc_ref, zt_ref, act_ref, *, ib, parts):
    step = pl.program_id(1)

    @pl.when(step == 0)
    def _():
        acc_ref[...] = jnp.zeros_like(acc_ref)

    hbt = pltpu.bitcast(ht_ref[...], BF16)
    p = hbt.shape[1]
    per_part = ib // parts
    width = per_part * PEER_NKEYS
    for part in range(parts):
        rows = slice(part * width, (part + 1) * width)
        u_part = pltpu.bitcast(u_ref[part * width // 2:(part + 1) * width // 2, :], BF16)
        zt_ref[rows, :] = jnp.dot(u_part, hbt, preferred_element_type=F32)
    def gate_pair(first):
        iis = (first, first + 1)
        for lt in range(p // LANES):
            ls = slice(lt * LANES, (lt + 1) * LANES)
            gs = [jnp.zeros((PEER_NKEYS // BF16_ROWS, BF16_ROWS, LANES), BF16) for _ in iis]
            for h in range(PEER_HEADS):
                r1 = pltpu.bitcast(r1_ref[h, :, ls], BF16).reshape(gs[0].shape)
                e1 = pltpu.bitcast(e1_ref[h, :, ls], BF16).reshape(gs[0].shape)
                for k, ii in enumerate(iis):
                    cnt = jnp.broadcast_to(cnt_ref[h, ii:ii + 1, ls], (BF16_ROWS, LANES)).astype(BF16)[None]
                    e0 = jnp.broadcast_to(e0_ref[h, ii:ii + 1, ls], (BF16_ROWS, LANES)).astype(BF16)[None]
                    gs[k] = gs[k] + e0 * jnp.where(r1 < cnt, e1, jnp.zeros((), BF16))
            for k, ii in enumerate(iis):
                rows = slice(ii * PEER_NKEYS, (ii + 1) * PEER_NKEYS)
                z = zt_ref[rows, ls]
                zg = (z * (1.0 + lax.erf(z * (2.0 ** -0.5)))).astype(BF16)
                act_ref[rows, ls] = zg * gs[k].reshape(PEER_NKEYS, LANES)

    for part in range(parts):
        for first in range(part * per_part, (part + 1) * per_part, 2):
            gate_pair(first)
        rows = slice(part * width, (part + 1) * width)
        acc_ref[...] += jnp.dot(pltpu.bitcast(vt_ref[:, rows], BF16), act_ref[rows, :],
                                preferred_element_type=F32)

    @pl.when(step == pl.num_programs(1) - 1)
    def _():
        o_ref[...] = x_ref[...] + gt_ref[0] * acc_ref[...].T


def _experts(h2t, u_packed, vt_packed, layer, r1, e1, cnt, e0, x, gt2, seq):
    n, d = x.shape
    p = min(1024, seq)
    ib = 8
    tpb = seq // p
    blk = ib * PEER_NKEYS
    stat = pl.BlockSpec((PEER_HEADS, PEER_NKEYS // 2, p), lambda i, j: (0, 0, i))
    per_i = pl.BlockSpec((PEER_HEADS, ib, p), lambda i, j: (0, j, i))
    return pl.pallas_call(
        functools.partial(_experts_kernel, ib=ib, parts=2),
        grid=(n // p, PEER_NKEYS // ib),
        in_specs=[
            pl.BlockSpec((d // 2, p), lambda i, j: (0, i)),
            pl.BlockSpec((None, blk // 2, d), lambda i, j: (layer, j, 0)),
            pl.BlockSpec((None, d // 2, blk), lambda i, j: (layer, 0, j)),
            stat, stat, per_i, per_i,
            pl.BlockSpec((p, d), lambda i, j: (i, 0)),
            pl.BlockSpec((1, 1, d), lambda i, j: (i // tpb, 0, 0)),
        ],
        out_specs=pl.BlockSpec((p, d), lambda i, j: (i, 0)),
        out_shape=jax.ShapeDtypeStruct((n, d), F32),
        scratch_shapes=[pltpu.VMEM((d, p), F32), pltpu.VMEM((blk, p), F32), pltpu.VMEM((blk, p), BF16)],
        compiler_params=_cparams(2),
        name="peer_experts",
    )(h2t, u_packed, vt_packed, r1, e1, cnt, e0, x, gt2)


def _pack_tables_kernel(u_ref, v_ref, uo_ref, vo_ref):
    uo_ref[...] = pltpu.bitcast(u_ref[...].astype(BF16), jnp.uint32)
    vo_ref[...] = pltpu.bitcast(v_ref[...].T.astype(BF16), jnp.uint32)


def _pack_tables(peer_u, peer_v):
    depth, experts, d = peer_u.shape
    te = 1024
    return pl.pallas_call(
        _pack_tables_kernel,
        grid=(depth, experts // te),
        in_specs=[
            pl.BlockSpec((None, te, d), lambda l, j: (l, j, 0)),
            pl.BlockSpec((None, te, d), lambda l, j: (l, j, 0)),
        ],
        out_specs=[
            pl.BlockSpec((None, te // 2, d), lambda l, j: (l, j, 0)),
            pl.BlockSpec((None, d // 2, te), lambda l, j: (l, 0, j)),
        ],
        out_shape=[
            jax.ShapeDtypeStruct((depth, experts // 2, d), jnp.uint32),
            jax.ShapeDtypeStruct((depth, d // 2, experts), jnp.uint32),
        ],
        compiler_params=_cparams(2),
        name="pack_tables",
    )(peer_u, peer_v)


def _final_kernel(x_ref, g_ref, o_ref):
    o_ref[...] = _rms(x_ref[...], g_ref[...])


def _final_norm(x, g):
    n, d = x.shape
    tm = 1024
    return pl.pallas_call(
        _final_kernel,
        grid=(n // tm,),
        in_specs=[pl.BlockSpec((tm, d), lambda i: (i, 0)), pl.BlockSpec((1, d), lambda i: (0, 0))],
        out_specs=pl.BlockSpec((tm, d), lambda i: (i, 0)),
        out_shape=jax.ShapeDtypeStruct((n, d), F32),
        compiler_params=_cparams(1),
        name="final_norm",
    )(x, g)


def _lane_row(vals_lo, vals_hi):
    row = jnp.zeros((1, LANES), F32)
    row = row.at[0, 0:N_HEADS].set(vals_lo)
    return row.at[0, N_HEADS:2 * N_HEADS].set(vals_hi)


def kernel(x, c, ada_w, ada_b, norm1_g, norm2_g, final_g, w_in, hg_lb_logits, hg_norm_g, gdn_conv_w,
           gdn_a_log, gdn_dt_bias, gdn_norm_g, w_branch_hg, w_branch_gdn, w_out, peer_wq, peer_subkeys,
           peer_u, peer_v):
    batch, seq, d = x.shape
    depth = ada_w.shape[0]
    n = batch * seq

    sm = jax.nn.softmax(hg_lb_logits.astype(F32), axis=0)
    lower_bounds = jnp.cumsum(sm, axis=0) - sm[0:1]

    c_pad = jnp.zeros((SUBLANES, d), F32).at[:batch].set(c)
    mod = _adaln(c_pad, ada_w, ada_b)[:, :batch, :]

    u_packed, vt_packed = _pack_tables(peer_u, peer_v)
    tri = jnp.tril(jnp.ones((CHUNK, CHUNK), F32)).astype(BF16)
    zeros4 = jnp.zeros((N_HEADS,), F32)

    xf = x.reshape(n, d)
    for l in range(depth):
        sh1, sc1, gt1, sh2, sc2, gt2 = [m.reshape(batch, 1, d) for m in jnp.split(mod[l], 6, axis=-1)]
        w_l = w_in[l]
        k0 = 8 * BRANCH_W
        w_main = jnp.concatenate([w_l[:, :k0], w_l[:, k0 + 2 * N_HEADS:]], axis=1).astype(BF16)
        w_small = jnp.zeros((d, LANES), F32).at[:, :2 * N_HEADS].set(w_l[:, k0:k0 + 2 * N_HEADS]).astype(BF16)
        proj, ba = _proj(xf, norm1_g[l].reshape(1, d), sc1, sh1, w_main, w_small, seq)

        y_hg = _hgrn2(proj, lower_bounds[l].reshape(1, BRANCH_W), hg_norm_g[l].reshape(1, HEAD_DIM), tri,
                      batch, seq)
        y_gdn = _gdn(proj, ba, gdn_conv_w[l].astype(F32), _lane_row(zeros4, gdn_a_log[l].astype(F32)),
                     _lane_row(zeros4, gdn_dt_bias[l].astype(F32)), gdn_norm_g[l].reshape(1, HEAD_DIM), tri,
                     batch, seq)

        xf, h2t, q = _merge(y_hg, y_gdn, proj, xf, gt1, w_branch_hg[l].astype(BF16),
                            w_branch_gdn[l].astype(BF16), w_out[l].astype(BF16),
                            norm2_g[l].reshape(1, d), sc2, sh2, peer_wq[l].astype(BF16), seq)

        r1, e1, cnt, e0 = _route(q, peer_subkeys[l].astype(BF16))
        xf = _experts(h2t, u_packed, vt_packed, l, r1, e1, cnt, e0, xf, gt2, seq)

    return _final_norm(xf, final_g.reshape(1, d)).reshape(batch, seq, d)
```

```python
import functools

import jax
import jax.numpy as jnp
from jax import lax
from jax.experimental import pallas as pl
from jax.experimental.pallas import tpu as pltpu

F32 = jnp.float32
BF16 = jnp.bfloat16

D_MODEL = 1024
CHUNK = 64
N_HEADS = 4
HEAD_DIM = 128
BRANCH_W = N_HEADS * HEAD_DIM
CONV_K = 4
PEER_HEADS = 8
PEER_NKEYS = 128
PEER_HALF = 128
PEER_TOPK = 16
EPS = 1e-6
NEG = -1e30
EXP_CLAMP = 60.0
RANK_BASE = 1e30
RANK_STEP = 1e26

LANES = 128
SUBLANES = 8
BF16_ROWS = 16
VMEM_LIMIT = 56 * 1024 * 1024


def _cparams(n_axes):
    return pltpu.CompilerParams(
        dimension_semantics=("arbitrary",) * n_axes, vmem_limit_bytes=VMEM_LIMIT)


def _sigmoid(x):
    return 1.0 / (1.0 + jnp.exp(-x))


def _silu(x):
    return x * _sigmoid(x)


def _dot(a, b):
    return jnp.dot(a.astype(BF16), b.astype(BF16), preferred_element_type=F32)


def _dot_nt(a, b):
    return lax.dot_general(a.astype(BF16), b.astype(BF16), (((1,), (1,)), ((), ())),
                           preferred_element_type=F32)


def _dot_tn(a, b):
    return lax.dot_general(a.astype(BF16), b.astype(BF16), (((0,), (0,)), ((), ())),
                           preferred_element_type=F32)


def _mm(a, b):
    return _dot(a, b)


def _dot_split3(m_bf16, x):
    h1 = x.astype(BF16)
    r1 = x - h1.astype(F32)
    h2 = r1.astype(BF16)
    h3 = (r1 - h2.astype(F32)).astype(BF16)
    d = lambda h: jnp.dot(m_bf16, h, preferred_element_type=F32)
    return d(h1) + d(h2) + d(h3)


def _rms(x, g):
    return x * lax.rsqrt(jnp.mean(x * x, axis=-1, keepdims=True) + EPS) * g


def _adaln_kernel(c_ref, w_ref, b_ref, o_ref):
    cond = _silu(c_ref[...])
    o_ref[0] = jnp.dot(cond, w_ref[0], precision=lax.Precision.HIGHEST,
                       preferred_element_type=F32) + b_ref[0]


def _adaln(c_pad, ada_w, ada_b):
    depth, d, cols = ada_w.shape
    tn = 1536
    return pl.pallas_call(
        _adaln_kernel,
        grid=(depth, cols // tn),
        in_specs=[
            pl.BlockSpec((SUBLANES, d), lambda l, j: (0, 0)),
            pl.BlockSpec((1, d, tn), lambda l, j: (l, 0, j)),
            pl.BlockSpec((1, 1, tn), lambda l, j: (l, 0, j)),
        ],
        out_specs=pl.BlockSpec((1, SUBLANES, tn), lambda l, j: (l, 0, j)),
        out_shape=jax.ShapeDtypeStruct((depth, SUBLANES, cols), F32),
        compiler_params=_cparams(2),
        name="adaln",
    )(c_pad, ada_w, ada_b.reshape(depth, 1, cols))


def _proj_kernel(x_ref, g_ref, sc_ref, sh_ref, w_ref, ws_ref, o_ref, os_ref, h_ref):
    @pl.when(pl.program_id(1) == 0)
    def _():
        h = _rms(x_ref[...], g_ref[...]) * (1.0 + sc_ref[0]) + sh_ref[0]
        hb = h.astype(BF16)
        h_ref[...] = hb
        os_ref[...] = jnp.dot(hb, ws_ref[...], preferred_element_type=F32)

    o_ref[...] = jnp.dot(h_ref[...], w_ref[...], preferred_element_type=F32)


def _proj(x, g, sc, sh, w_main, w_small, seq):
    n, d = x.shape
    cols = w_main.shape[1]
    tm, tn = 1024, 2048
    tpb = seq // tm
    return pl.pallas_call(
        _proj_kernel,
        grid=(n // tm, cols // tn),
        in_specs=[
            pl.BlockSpec((tm, d), lambda i, j: (i, 0)),
            pl.BlockSpec((1, d), lambda i, j: (0, 0)),
            pl.BlockSpec((1, 1, d), lambda i, j: (i // tpb, 0, 0)),
            pl.BlockSpec((1, 1, d), lambda i, j: (i // tpb, 0, 0)),
            pl.BlockSpec((d, tn), lambda i, j: (0, j)),
            pl.BlockSpec((d, LANES), lambda i, j: (0, 0)),
        ],
        out_specs=[
            pl.BlockSpec((tm, tn), lambda i, j: (i, j)),
            pl.BlockSpec((tm, LANES), lambda i, j: (i, 0)),
        ],
        out_shape=[
            jax.ShapeDtypeStruct((n, cols), F32),
            jax.ShapeDtypeStruct((n, LANES), F32),
        ],
        scratch_shapes=[pltpu.VMEM((tm, d), BF16)],
        compiler_params=_cparams(2),
        name="in_proj",
    )(x, g, sc, sh, w_main, w_small)


def _hgrn2_kernel(p_ref, lb_ref, ng_ref, tri_ref, y_ref, st_ref, *, n_chunks):
    @pl.when(pl.program_id(1) == 0)
    def _():
        st_ref[...] = jnp.zeros_like(st_ref)

    q_ref, f_ref, i_ref, g_ref = (p_ref.at[:, k * BRANCH_W:(k + 1) * BRANCH_W] for k in range(4))

    lb = lb_ref[...]
    ng = ng_ref[...]
    tri = tri_ref[...]
    sub = CHUNK // 4
    row = lax.broadcasted_iota(jnp.int32, (CHUNK, CHUNK), 0)
    col = lax.broadcasted_iota(jnp.int32, (CHUNK, CHUNK), 1)
    causal = col <= row

    probs = [(c, h) for c in range(n_chunks) for h in range(N_HEADS)]
    b_chunks, f_chunks = [], []
    for c in range(n_chunks):
        rows = slice(c * CHUNK, (c + 1) * CHUNK)
        fgate = lb + (1.0 - lb) * _sigmoid(f_ref[rows, :])
        f_chunks.append(fgate)
        b_chunks.append(_dot_split3(tri, jnp.log(fgate)))
    bs, kks, qqs, vvs = [], [], [], []
    for c, h in probs:
        rows = slice(c * CHUNK, (c + 1) * CHUNK)
        hs = slice(h * HEAD_DIM, (h + 1) * HEAD_DIM)
        bs.append(b_chunks[c][:, hs])
        kks.append(1.0 - f_chunks[c][:, hs])
        qqs.append(_silu(q_ref[rows, hs]))
        vvs.append(i_ref[rows, hs])
    atts = []
    for b, kk, qq in zip(bs, kks, qqs):
        att_rows = []
        for blk in range(4):
            r0 = blk * sub
            ref_b = b[r0:r0 + 1, :]
            qe = qq[r0:r0 + sub, :] * jnp.exp(b[r0:r0 + sub, :] - ref_b)
            ke = kk * jnp.exp(jnp.minimum(ref_b - b, EXP_CLAMP))
            att_rows.append(_dot_nt(qe, ke))
        atts.append(jnp.where(causal, jnp.concatenate(att_rows, axis=0), 0.0))
    incs = [_dot_tn(vv, kk * jnp.exp(b[CHUNK - 1:CHUNK, :] - b)) for b, kk, vv in zip(bs, kks, vvs)]
    o_intra = [_dot(att, vv) for att, vv in zip(atts, vvs)]
    states = []
    cur = [st_ref[h] for h in range(N_HEADS)]
    for i, (c, h) in enumerate(probs):
        states.append(cur[h])
        cur[h] = cur[h] * jnp.exp(bs[i][CHUNK - 1:CHUNK, :]) + incs[i]
    for h in range(N_HEADS):
        st_ref[h] = cur[h]
    o_inter = [_dot_nt(qq * jnp.exp(b), st) for qq, b, st in zip(qqs, bs, states)]
    for i, (c, h) in enumerate(probs):
        rows = slice(c * CHUNK, (c + 1) * CHUNK)
        hs = slice(h * HEAD_DIM, (h + 1) * HEAD_DIM)
        y = _rms(o_inter[i] + o_intra[i], ng) * _silu(g_ref[rows, hs])
        y_ref[rows, hs] = y.astype(y_ref.dtype)


def _hgrn2(proj, lb, norm_g, tri, batch, seq):
    t = 256
    spb = seq // t
    col = lambda k: (lambda b, s: (b * spb + s, k))
    const = lambda b, s: (0, 0)
    return pl.pallas_call(
        functools.partial(_hgrn2_kernel, n_chunks=t // CHUNK),
        grid=(batch, spb),
        in_specs=[
            pl.BlockSpec((t, 4 * BRANCH_W), col(0)),
            pl.BlockSpec((1, BRANCH_W), const),
            pl.BlockSpec((1, HEAD_DIM), const),
            pl.BlockSpec((CHUNK, CHUNK), const),
        ],
        out_specs=pl.BlockSpec((t, BRANCH_W), lambda b, s: (b * spb + s, 0)),
        out_shape=jax.ShapeDtypeStruct((batch * seq, BRANCH_W), BF16),
        scratch_shapes=[pltpu.VMEM((N_HEADS, HEAD_DIM, HEAD_DIM), F32)],
        compiler_params=_cparams(2),
        name="hgrn2",
    )(proj, lb, norm_g, tri)


def _softplus(x):
    return jnp.maximum(x, 0.0) + jnp.log(1.0 + jnp.exp(-jnp.abs(x)))


def _gdn_kernel(p_ref, ba_ref, cw_ref, alog_ref, dtb_ref, ng_ref, tri_ref,
                y_ref, xb_ref, qn_ref, kn_ref, vn_ref, st_ref, *, t, n_chunks):
    first = pl.program_id(1) == 0
    q_ref, k_ref, v_ref, g_ref = (p_ref.at[:, k * BRANCH_W:(k + 1) * BRANCH_W] for k in range(4))

    @pl.when(first)
    def _():
        st_ref[...] = jnp.zeros_like(st_ref)
        xb_ref[:, 0:SUBLANES, :] = jnp.zeros((3, SUBLANES, BRANCH_W), F32)

    @pl.when(jnp.logical_not(first))
    def _():
        xb_ref[:, 0:SUBLANES, :] = xb_ref[:, t:t + SUBLANES, :]

    for j, (src, dst, scale) in enumerate(((q_ref, qn_ref, HEAD_DIM ** -0.5), (k_ref, kn_ref, 1.0),
                                           (v_ref, vn_ref, None))):
        xb_ref[j, SUBLANES:SUBLANES + t, :] = src[...]
        acc = jnp.zeros((t, BRANCH_W), F32)
        for tap in range(CONV_K):
            off = SUBLANES - (CONV_K - 1) + tap
            w = cw_ref[tap:tap + 1, j * BRANCH_W:(j + 1) * BRANCH_W]
            acc = acc + xb_ref[j, off:off + t, :] * w
        acc = _silu(acc)
        if scale is None:
            dst[...] = acc
        else:
            for h in range(N_HEADS):
                hs = slice(h * HEAD_DIM, (h + 1) * HEAD_DIM)
                xh = acc[:, hs]
                dst[:, hs] = xh * (lax.rsqrt(jnp.sum(xh * xh, axis=-1, keepdims=True) + EPS) * scale)

    tri = tri_ref[...]
    ng = ng_ref[...]
    row = lax.broadcasted_iota(jnp.int32, (CHUNK, CHUNK), 0)
    col = lax.broadcasted_iota(jnp.int32, (CHUNK, CHUNK), 1)
    causal = col <= row
    strict = col < row
    eye = (col == row).astype(F32)
    blk16 = (row // 16) == (col // 16)
    blk32 = (row // 32) == (col // 32)

    probs = [(c, h) for c in range(n_chunks) for h in range(N_HEADS)]
    qs, ks, vs, betas, bcols, decs = [], [], [], [], [], []
    for c in range(n_chunks):
        rows = slice(c * CHUNK, (c + 1) * CHUNK)
        ba = ba_ref[rows, :]
        beta_all = _sigmoid(ba)
        logdec = -jnp.exp(alog_ref[...]) * _softplus(ba + dtb_ref[...])
        b_all = _dot_split3(tri, logdec)
        b_pad = jnp.concatenate([b_all, jnp.zeros((LANES - CHUNK, LANES), F32)], axis=0)
        b_t = b_pad.T
        for h in range(N_HEADS):
            hs = slice(h * HEAD_DIM, (h + 1) * HEAD_DIM)
            lane = N_HEADS + h
            bcol = b_all[:, lane:lane + 1]
            brow = b_t[lane:lane + 1, 0:CHUNK]
            bcols.append(bcol)
            betas.append(beta_all[:, h:h + 1])
            decs.append(jnp.exp(jnp.where(causal, bcol - brow, NEG)))
            qs.append(qn_ref[rows, hs])
            ks.append(kn_ref[rows, hs])
            vs.append(vn_ref[rows, hs])
    kbs = [k * beta for k, beta in zip(ks, betas)]
    a_mats = [jnp.where(strict, _dot_nt(kb, k) * dec, 0.0) for kb, k, dec in zip(kbs, ks, decs)]
    aqks = [_dot_nt(q, k) * dec for q, k, dec in zip(qs, ks, decs)]
    a_diags = [jnp.where(blk16, a, 0.0) for a in a_mats]
    ps = [eye - a for a in a_diags]
    xs = [_mm(a, a) for a in a_diags]
    for step in range(3):
        pxs = [_mm(p, x) for p, x in zip(ps, xs)]
        if step < 2:
            xs = [_mm(x, x) for x in xs]
        ps = [p + px for p, px in zip(ps, pxs)]
    for low in ([jnp.where(blk32, a, 0.0) - d for a, d in zip(a_mats, a_diags)],
                [jnp.where(blk32, 0.0, a) for a in a_mats]):
        ms = [_mm(p, lo) for p, lo in zip(ps, low)]
        ps = [p - _mm(m, p) for p, m in zip(ps, ms)]
    rhss = [jnp.concatenate([v * beta, kb * jnp.exp(bcol)], axis=1)
            for v, beta, kb, bcol in zip(vs, betas, kbs, bcols)]
    uws = [rhs + _mm(jnp.where(strict, p, 0.0), rhs) for p, rhs in zip(ps, rhss)]
    wqs = [jnp.concatenate([uw[:, HEAD_DIM:], q * jnp.exp(bcol)], axis=0)
           for uw, q, bcol in zip(uws, qs, bcols)]
    kds = [k * jnp.exp(bcol[CHUNK - 1:CHUNK, :] - bcol) for k, bcol in zip(ks, bcols)]
    cur = [st_ref[h] for h in range(N_HEADS)]
    outs = [None] * len(probs)
    for c in range(n_chunks):
        idx = [c * N_HEADS + h for h in range(N_HEADS)]
        wss = [_dot(wqs[i], cur[h]) for h, i in enumerate(idx)]
        v_news = [uws[i][:, :HEAD_DIM] - ws[:CHUNK] for i, ws in zip(idx, wss)]
        o_new = [_dot(aqks[i], v_new) for i, v_new in zip(idx, v_news)]
        incs = [_dot_tn(kds[i], v_new) for i, v_new in zip(idx, v_news)]
        for h, i in enumerate(idx):
            outs[i] = wss[h][CHUNK:] + o_new[h]
            cur[h] = cur[h] * jnp.exp(bcols[i][CHUNK - 1:CHUNK, :]) + incs[h]
    for h in range(N_HEADS):
        st_ref[h] = cur[h]
    for i, (c, h) in enumerate(probs):
        rows = slice(c * CHUNK, (c + 1) * CHUNK)
        hs = slice(h * HEAD_DIM, (h + 1) * HEAD_DIM)
        y = _rms(outs[i], ng) * _silu(g_ref[rows, hs])
        y_ref[rows, hs] = y.astype(y_ref.dtype)


def _gdn(proj, ba, conv_w, alog_row, dtb_row, norm_g, tri, batch, seq):
    t = 256
    spb = seq // t
    col = lambda k: (lambda b, s: (b * spb + s, k))
    const = lambda b, s: (0, 0)
    return pl.pallas_call(
        functools.partial(_gdn_kernel, t=t, n_chunks=t // CHUNK),
        grid=(batch, spb),
        in_specs=[
            pl.BlockSpec((t, 4 * BRANCH_W), col(1)),
            pl.BlockSpec((t, LANES), col(0)),
            pl.BlockSpec((CONV_K, 3 * BRANCH_W), const),
            pl.BlockSpec((1, LANES), const),
            pl.BlockSpec((1, LANES), const),
            pl.BlockSpec((1, HEAD_DIM), const),
            pl.BlockSpec((CHUNK, CHUNK), const),
        ],
        out_specs=pl.BlockSpec((t, BRANCH_W), lambda b, s: (b * spb + s, 0)),
        out_shape=jax.ShapeDtypeStruct((batch * seq, BRANCH_W), BF16),
        scratch_shapes=[
            pltpu.VMEM((3, t + 2 * SUBLANES, BRANCH_W), F32),
            pltpu.VMEM((t, BRANCH_W), F32),
            pltpu.VMEM((t, BRANCH_W), F32),
            pltpu.VMEM((t, BRANCH_W), F32),
            pltpu.VMEM((N_HEADS, HEAD_DIM, HEAD_DIM), F32),
        ],
        compiler_params=_cparams(2),
        name="gdn",
    )(proj, ba, conv_w, alog_row, dtb_row, norm_g, tri)


def _merge_kernel(yh_ref, yg_ref, gate_ref, x_ref, gt_ref, wbh_ref, wbg_ref, wo_ref,
                  g2_ref, sc_ref, sh_ref, wq_ref, xo_ref, h2t_ref, q_ref):
    d = x_ref.shape[1]
    merged = (_sigmoid(gate_ref[:, :d]) * jnp.dot(yh_ref[...], wbh_ref[...], preferred_element_type=F32)
              + _sigmoid(gate_ref[:, d:]) * jnp.dot(yg_ref[...], wbg_ref[...], preferred_element_type=F32))
    y = jnp.dot(merged.astype(BF16), wo_ref[...], preferred_element_type=F32)
    xn = x_ref[...] + gt_ref[0] * y
    xo_ref[...] = xn
    h2 = _rms(xn, g2_ref[...]) * (1.0 + sc_ref[0]) + sh_ref[0]
    h2t_ref[...] = pltpu.bitcast(h2.T.astype(BF16), jnp.uint32)
    q_ref[...] = jnp.dot(h2.astype(BF16), wq_ref[...], preferred_element_type=F32).astype(q_ref.dtype)


def _merge(y_hg, y_gdn, proj, x, gt1, w_bh, w_bg, w_out, g2, sc2, sh2, wq, seq):
    n, d = x.shape
    tm = 512
    tpb = seq // tm
    qw = wq.shape[1]
    const = lambda i: (0, 0)
    per_b = lambda i: (i // tpb, 0, 0)
    gate_blk = 8 * BRANCH_W // (2 * D_MODEL)
    return pl.pallas_call(
        _merge_kernel,
        grid=(n // tm,),
        in_specs=[
            pl.BlockSpec((tm, BRANCH_W), lambda i: (i, 0)),
            pl.BlockSpec((tm, BRANCH_W), lambda i: (i, 0)),
            pl.BlockSpec((tm, 2 * d), lambda i: (i, gate_blk)),
            pl.BlockSpec((tm, d), lambda i: (i, 0)),
            pl.BlockSpec((1, 1, d), per_b),
            pl.BlockSpec((BRANCH_W, d), const),
            pl.BlockSpec((BRANCH_W, d), const),
            pl.BlockSpec((d, d), const),
            pl.BlockSpec((1, d), const),
            pl.BlockSpec((1, 1, d), per_b),
            pl.BlockSpec((1, 1, d), per_b),
            pl.BlockSpec((d, qw), const),
        ],
        out_specs=[
            pl.BlockSpec((tm, d), lambda i: (i, 0)),
            pl.BlockSpec((d // 2, tm), lambda i: (0, i)),
            pl.BlockSpec((tm, qw), lambda i: (i, 0)),
        ],
        out_shape=[
            jax.ShapeDtypeStruct((n, d), F32),
            jax.ShapeDtypeStruct((d // 2, n), jnp.uint32),
            jax.ShapeDtypeStruct((n, qw), BF16),
        ],
        compiler_params=_cparams(1),
        name="merge",
    )(y_hg, y_gdn, proj, x, gt1, w_bh, w_bg, w_out, g2, sc2, sh2, wq)


def _top_ranked(s, k):
    rid = lax.broadcasted_iota(jnp.int32, (k, s.shape[1]), 0)
    vals = jnp.zeros((k, s.shape[1]), F32)
    for r in range(k):
        m = jnp.max(s, axis=0, keepdims=True)
        vals = jnp.where(rid == r, m, vals)
        s = jnp.where(s == m, -(RANK_BASE + r * RANK_STEP), s)
    rank = jnp.where(s < -0.5 * RANK_BASE, jnp.round((-s - RANK_BASE) * (1.0 / RANK_STEP)), float(k))
    return vals, rank


def _route_kernel(q_ref, sk_ref, r1_ref, e1_ref, cnt_ref, e0_ref):
    p = q_ref.shape[0]
    kk = PEER_TOPK
    rid = lax.broadcasted_iota(jnp.int32, (SUBLANES, p), 0)
    for h in range(PEER_HEADS):
        q0 = q_ref[:, (2 * h) * PEER_HALF:(2 * h + 1) * PEER_HALF]
        q1 = q_ref[:, (2 * h + 1) * PEER_HALF:(2 * h + 2) * PEER_HALF]
        s0 = _dot_nt(sk_ref[h, 0], q0)
        s1 = _dot_nt(sk_ref[h, 1], q1)
        v0, rank0 = _top_ranked(s0, kk)
        v1, rank1 = _top_ranked(s1, kk)
        lo, hi = v1[0:SUBLANES, :], v1[SUBLANES:kk, :]
        pieces = [v0[0:1, :] + lo, v0[0:1, :] + hi, v0[1:2, :] + lo]
        for r0 in range(2, SUBLANES):
            pieces.append(jnp.where(rid < kk // (r0 + 1), v0[r0:r0 + 1, :] + lo, NEG))
        pieces.append(v0[SUBLANES:kk, :] + v1[0:1, :])
        work = list(pieces)
        tops = []
        for _ in range(kk):
            m = work[0]
            for a in work[1:]:
                m = jnp.maximum(m, a)
            m = jnp.max(m, axis=0, keepdims=True)
            tops.append(m)
            work = [jnp.where(a == m, NEG, a) for a in work]
        tau = tops[kk - 1]
        zsum = jnp.zeros_like(tau)
        for tk in tops:
            zsum = zsum + jnp.exp(tk - tops[0])
        sel = [jnp.where(pc >= tau, 1.0, 0.0) for pc in pieces]
        n_sel = [jnp.sum(sel[0] + sel[1], axis=0, keepdims=True)]
        n_sel += [jnp.sum(sel[r0 + 1], axis=0, keepdims=True) for r0 in range(1, SUBLANES)]
        n_high = jnp.sum(sel[SUBLANES + 1], axis=0, keepdims=True)
        high = (rank0 >= float(SUBLANES)) & (rank0 < float(SUBLANES) + n_high)
        cnt = jnp.where(high, 1.0, 0.0)
        for r0 in range(SUBLANES):
            cnt = jnp.where(rank0 == float(r0), n_sel[r0], cnt)
        r1_ref[h] = pltpu.bitcast(rank1.astype(BF16), jnp.uint32)
        e1_ref[h] = pltpu.bitcast(jnp.exp(s1 - v1[0:1, :]).astype(BF16), jnp.uint32)
        cnt_ref[h] = cnt
        e0_ref[h] = jnp.exp(s0 - v0[0:1, :]) * (0.5 / zsum)


def _route(q, subkeys_bf16):
    n, qw = q.shape
    p = LANES
    shp = jax.ShapeDtypeStruct((PEER_HEADS, PEER_NKEYS, n), F32)
    spec = pl.BlockSpec((PEER_HEADS, PEER_NKEYS, p), lambda i: (0, 0, i))
    shp_packed = jax.ShapeDtypeStruct((PEER_HEADS, PEER_NKEYS // 2, n), jnp.uint32)
    spec_packed = pl.BlockSpec((PEER_HEADS, PEER_NKEYS // 2, p), lambda i: (0, 0, i))
    return pl.pallas_call(
        _route_kernel,
        grid=(n // p,),
        in_specs=[
            pl.BlockSpec((p, qw), lambda i: (i, 0)),
            pl.BlockSpec((PEER_HEADS, 2, PEER_NKEYS, PEER_HALF), lambda i: (0, 0, 0, 0)),
        ],
        out_specs=[spec_packed, spec_packed, spec, spec],
        out_shape=[shp_packed, shp_packed, shp, shp],
        compiler_params=_cparams(1),
        name="peer_route",
    )(q, subkeys_bf16)


def _experts_kernel(ht_ref, u_ref, vt_ref, r1_ref, e1_ref, cnt_ref, e0_ref, x_ref, gt_ref, fg_ref, o_ref,
                    acc_ref, zt_ref, act_ref, *, ib, parts, final):
    step = pl.program_id(1)

    @pl.when(step == 0)
    def _():
        acc_ref[...] = jnp.zeros_like(acc_ref)

    hbt = pltpu.bitcast(ht_ref[...], BF16)
    p = hbt.shape[1]
    per_part = ib // parts
    width = per_part * PEER_NKEYS
    for part in range(parts):
        rows = slice(part * width, (part + 1) * width)
        u_part = pltpu.bitcast(u_ref[part * width // 2:(part + 1) * width // 2, :], BF16)
        zt_ref[rows, :] = jnp.dot(u_part, hbt, preferred_element_type=F32)
    def gate_pair(first):
        iis = (first, first + 1)
        for lt in range(p // LANES):
            ls = slice(lt * LANES, (lt + 1) * LANES)
            gs = [jnp.zeros((PEER_NKEYS // BF16_ROWS, BF16_ROWS, LANES), BF16) for _ in iis]
            for h in range(PEER_HEADS):
                r1 = pltpu.bitcast(r1_ref[h, :, ls], BF16).reshape(gs[0].shape)
                e1 = pltpu.bitcast(e1_ref[h, :, ls], BF16).reshape(gs[0].shape)
                for k, ii in enumerate(iis):
                    cnt = jnp.broadcast_to(cnt_ref[h, ii:ii + 1, ls], (BF16_ROWS, LANES)).astype(BF16)[None]
                    e0 = jnp.broadcast_to(e0_ref[h, ii:ii + 1, ls], (BF16_ROWS, LANES)).astype(BF16)[None]
                    gs[k] = gs[k] + e0 * jnp.where(r1 < cnt, e1, jnp.zeros((), BF16))
            for k, ii in enumerate(iis):
                rows = slice(ii * PEER_NKEYS, (ii + 1) * PEER_NKEYS)
                z = zt_ref[rows, ls]
                zg = (z * (1.0 + lax.erf(z * (2.0 ** -0.5)))).astype(BF16)
                act_ref[rows, ls] = zg * gs[k].reshape(PEER_NKEYS, LANES)

    for part in range(parts):
        for first in range(part * per_part, (part + 1) * per_part, 2):
            gate_pair(first)
        rows = slice(part * width, (part + 1) * width)
        acc_ref[...] += jnp.dot(pltpu.bitcast(vt_ref[:, rows], BF16), act_ref[rows, :],
                                preferred_element_type=F32)

    @pl.when(step == pl.num_programs(1) - 1)
    def _():
        xn = x_ref[...] + gt_ref[0] * acc_ref[...].T
        o_ref[...] = _rms(xn, fg_ref[...]) if final else xn


def _experts(h2t, u_packed, vt_packed, layer, r1, e1, cnt, e0, x, gt2, final_g, final, seq):
    n, d = x.shape
    p = min(1024, seq)
    ib = 8
    tpb = seq // p
    blk = ib * PEER_NKEYS
    stat = pl.BlockSpec((PEER_HEADS, PEER_NKEYS // 2, p), lambda i, j: (0, 0, i))
    per_i = pl.BlockSpec((PEER_HEADS, ib, p), lambda i, j: (0, j, i))
    return pl.pallas_call(
        functools.partial(_experts_kernel, ib=ib, parts=2, final=final),
        grid=(n // p, PEER_NKEYS // ib),
        in_specs=[
            pl.BlockSpec((d // 2, p), lambda i, j: (0, i)),
            pl.BlockSpec((None, blk // 2, d), lambda i, j: (layer, j, 0)),
            pl.BlockSpec((None, d // 2, blk), lambda i, j: (layer, 0, j)),
            stat, stat, per_i, per_i,
            pl.BlockSpec((p, d), lambda i, j: (i, 0)),
            pl.BlockSpec((1, 1, d), lambda i, j: (i // tpb, 0, 0)),
            pl.BlockSpec((1, d), lambda i, j: (0, 0)),
        ],
        out_specs=pl.BlockSpec((p, d), lambda i, j: (i, 0)),
        out_shape=jax.ShapeDtypeStruct((n, d), F32),
        scratch_shapes=[pltpu.VMEM((d, p), F32), pltpu.VMEM((blk, p), F32), pltpu.VMEM((blk, p), BF16)],
        compiler_params=_cparams(2),
        name="peer_experts",
    )(h2t, u_packed, vt_packed, r1, e1, cnt, e0, x, gt2, final_g)


def _pack_tables_kernel(u_ref, v_ref, uo_ref, vo_ref):
    uo_ref[...] = pltpu.bitcast(u_ref[...].astype(BF16), jnp.uint32)
    vo_ref[...] = pltpu.bitcast(v_ref[...].T.astype(BF16), jnp.uint32)


def _pack_tables(peer_u, peer_v):
    depth, experts, d = peer_u.shape
    te = 1024
    return pl.pallas_call(
        _pack_tables_kernel,
        grid=(depth, experts // te),
        in_specs=[
            pl.BlockSpec((None, te, d), lambda l, j: (l, j, 0)),
            pl.BlockSpec((None, te, d), lambda l, j: (l, j, 0)),
        ],
        out_specs=[
            pl.BlockSpec((None, te // 2, d), lambda l, j: (l, j, 0)),
            pl.BlockSpec((None, d // 2, te), lambda l, j: (l, 0, j)),
        ],
        out_shape=[
            jax.ShapeDtypeStruct((depth, experts // 2, d), jnp.uint32),
            jax.ShapeDtypeStruct((depth, d // 2, experts), jnp.uint32),
        ],
        compiler_params=_cparams(2),
        name="pack_tables",
    )(peer_u, peer_v)


def _lane_row(vals_lo, vals_hi):
    row = jnp.zeros((1, LANES), F32)
    row = row.at[0, 0:N_HEADS].set(vals_lo)
    return row.at[0, N_HEADS:2 * N_HEADS].set(vals_hi)


def kernel(x, c, ada_w, ada_b, norm1_g, norm2_g, final_g, w_in, hg_lb_logits, hg_norm_g, gdn_conv_w,
           gdn_a_log, gdn_dt_bias, gdn_norm_g, w_branch_hg, w_branch_gdn, w_out, peer_wq, peer_subkeys,
           peer_u, peer_v):
    batch, seq, d = x.shape
    depth = ada_w.shape[0]
    n = batch * seq

    sm = jax.nn.softmax(hg_lb_logits.astype(F32), axis=0)
    lower_bounds = jnp.cumsum(sm, axis=0) - sm[0:1]

    c_pad = jnp.zeros((SUBLANES, d), F32).at[:batch].set(c)
    mod = _adaln(c_pad, ada_w, ada_b)[:, :batch, :]

    u_packed, vt_packed = _pack_tables(peer_u, peer_v)
    tri = jnp.tril(jnp.ones((CHUNK, CHUNK), F32)).astype(BF16)
    zeros4 = jnp.zeros((N_HEADS,), F32)

    xf = x.reshape(n, d)
    for l in range(depth):
        sh1, sc1, gt1, sh2, sc2, gt2 = [m.reshape(batch, 1, d) for m in jnp.split(mod[l], 6, axis=-1)]
        w_l = w_in[l]
        k0 = 8 * BRANCH_W
        w_main = jnp.concatenate([w_l[:, :k0], w_l[:, k0 + 2 * N_HEADS:]], axis=1).astype(BF16)
        w_small = jnp.zeros((d, LANES), F32).at[:, :2 * N_HEADS].set(w_l[:, k0:k0 + 2 * N_HEADS]).astype(BF16)
        proj, ba = _proj(xf, norm1_g[l].reshape(1, d), sc1, sh1, w_main, w_small, seq)

        y_hg = _hgrn2(proj, lower_bounds[l].reshape(1, BRANCH_W), hg_norm_g[l].reshape(1, HEAD_DIM), tri,
                      batch, seq)
        y_gdn = _gdn(proj, ba, gdn_conv_w[l].astype(F32), _lane_row(zeros4, gdn_a_log[l].astype(F32)),
                     _lane_row(zeros4, gdn_dt_bias[l].astype(F32)), gdn_norm_g[l].reshape(1, HEAD_DIM), tri,
                     batch, seq)

        xf, h2t, q = _merge(y_hg, y_gdn, proj, xf, gt1, w_branch_hg[l].astype(BF16),
                            w_branch_gdn[l].astype(BF16), w_out[l].astype(BF16),
                            norm2_g[l].reshape(1, d), sc2, sh2, peer_wq[l].astype(BF16), seq)

        r1, e1, cnt, e0 = _route(q, peer_subkeys[l].astype(BF16))
        xf = _experts(h2t, u_packed, vt_packed, l, r1, e1, cnt, e0, xf, gt2, final_g.reshape(1, d),
                      l == depth - 1, seq)

    return xf.reshape(batch, seq, d)
```

```python
import functools

import jax
import jax.numpy as jnp
from jax import lax
from jax.experimental import pallas as pl
from jax.experimental.pallas import tpu as pltpu

F32 = jnp.float32
BF16 = jnp.bfloat16

D_MODEL = 1024
CHUNK = 64
N_HEADS = 4
HEAD_DIM = 128
BRANCH_W = N_HEADS * HEAD_DIM
CONV_K = 4
PEER_HEADS = 8
PEER_NKEYS = 128
PEER_HALF = 128
PEER_TOPK = 16
EPS = 1e-6
NEG = -1e30
EXP_CLAMP = 60.0
RANK_BASE = 1e30
RANK_STEP = 1e26

LANES = 128
SUBLANES = 8
BF16_ROWS = 16
VMEM_LIMIT = 56 * 1024 * 1024


def _cparams(n_axes):
    return pltpu.CompilerParams(
        dimension_semantics=("arbitrary",) * n_axes, vmem_limit_bytes=VMEM_LIMIT)


def _sigmoid(x):
    return 1.0 / (1.0 + jnp.exp(-x))


def _silu(x):
    return x * _sigmoid(x)


def _dot(a, b):
    return jnp.dot(a.astype(BF16), b.astype(BF16), preferred_element_type=F32)


def _dot_nt(a, b):
    return lax.dot_general(a.astype(BF16), b.astype(BF16), (((1,), (1,)), ((), ())),
                           preferred_element_type=F32)


def _dot_tn(a, b):
    return lax.dot_general(a.astype(BF16), b.astype(BF16), (((0,), (0,)), ((), ())),
                           preferred_element_type=F32)


def _mm(a, b):
    return _dot(a, b)


def _dot_split3(m_bf16, x):
    h1 = x.astype(BF16)
    r1 = x - h1.astype(F32)
    h2 = r1.astype(BF16)
    h3 = (r1 - h2.astype(F32)).astype(BF16)
    d = lambda h: jnp.dot(m_bf16, h, preferred_element_type=F32)
    return d(h1) + d(h2) + d(h3)


def _rms(x, g):
    return x * lax.rsqrt(jnp.mean(x * x, axis=-1, keepdims=True) + EPS) * g


def _adaln_kernel(c_ref, w_ref, b_ref, o_ref):
    cond = _silu(c_ref[...])
    o_ref[0] = jnp.dot(cond, w_ref[0], precision=lax.Precision.HIGHEST,
                       preferred_element_type=F32) + b_ref[0]


def _adaln(c_pad, ada_w, ada_b):
    depth, d, cols = ada_w.shape
    tn = 1536
    return pl.pallas_call(
        _adaln_kernel,
        grid=(depth, cols // tn),
        in_specs=[
            pl.BlockSpec((SUBLANES, d), lambda l, j: (0, 0)),
            pl.BlockSpec((1, d, tn), lambda l, j: (l, 0, j)),
            pl.BlockSpec((1, 1, tn), lambda l, j: (l, 0, j)),
        ],
        out_specs=pl.BlockSpec((1, SUBLANES, tn), lambda l, j: (l, 0, j)),
        out_shape=jax.ShapeDtypeStruct((depth, SUBLANES, cols), F32),
        compiler_params=_cparams(2),
        name="adaln",
    )(c_pad, ada_w, ada_b.reshape(depth, 1, cols))


def _proj_kernel(x_ref, g_ref, sc_ref, sh_ref, w_ref, ws_ref, o_ref, os_ref, h_ref):
    @pl.when(pl.program_id(1) == 0)
    def _():
        h = _rms(x_ref[...], g_ref[...]) * (1.0 + sc_ref[0]) + sh_ref[0]
        hb = h.astype(BF16)
        h_ref[...] = hb
        os_ref[...] = jnp.dot(hb, ws_ref[...], preferred_element_type=F32)

    o_ref[...] = jnp.dot(h_ref[...], w_ref[...], preferred_element_type=F32)


def _proj(x, g, sc, sh, w_main, w_small, seq):
    n, d = x.shape
    cols = w_main.shape[1]
    tm, tn = 1024, 2048
    tpb = seq // tm
    return pl.pallas_call(
        _proj_kernel,
        grid=(n // tm, cols // tn),
        in_specs=[
            pl.BlockSpec((tm, d), lambda i, j: (i, 0)),
            pl.BlockSpec((1, d), lambda i, j: (0, 0)),
            pl.BlockSpec((1, 1, d), lambda i, j: (i // tpb, 0, 0)),
            pl.BlockSpec((1, 1, d), lambda i, j: (i // tpb, 0, 0)),
            pl.BlockSpec((d, tn), lambda i, j: (0, j)),
            pl.BlockSpec((d, LANES), lambda i, j: (0, 0)),
        ],
        out_specs=[
            pl.BlockSpec((tm, tn), lambda i, j: (i, j)),
            pl.BlockSpec((tm, LANES), lambda i, j: (i, 0)),
        ],
        out_shape=[
            jax.ShapeDtypeStruct((n, cols), F32),
            jax.ShapeDtypeStruct((n, LANES), F32),
        ],
        scratch_shapes=[pltpu.VMEM((tm, d), BF16)],
        compiler_params=_cparams(2),
        name="in_proj",
    )(x, g, sc, sh, w_main, w_small)


def _hgrn2_kernel(p_ref, lb_ref, ng_ref, tri_ref, y_ref, st_ref, *, n_chunks):
    @pl.when(pl.program_id(1) == 0)
    def _():
        st_ref[...] = jnp.zeros_like(st_ref)

    q_ref, f_ref, i_ref, g_ref = (p_ref.at[:, k * BRANCH_W:(k + 1) * BRANCH_W] for k in range(4))

    lb = lb_ref[...]
    ng = ng_ref[...]
    tri = tri_ref[...]
    sub = CHUNK // 4
    row = lax.broadcasted_iota(jnp.int32, (CHUNK, CHUNK), 0)
    col = lax.broadcasted_iota(jnp.int32, (CHUNK, CHUNK), 1)
    causal = col <= row

    probs = [(c, h) for c in range(n_chunks) for h in range(N_HEADS)]
    b_chunks, f_chunks = [], []
    for c in range(n_chunks):
        rows = slice(c * CHUNK, (c + 1) * CHUNK)
        fgate = lb + (1.0 - lb) * _sigmoid(f_ref[rows, :])
        f_chunks.append(fgate)
        b_chunks.append(_dot_split3(tri, jnp.log(fgate)))
    bs, kks, qqs, vvs = [], [], [], []
    for c, h in probs:
        rows = slice(c * CHUNK, (c + 1) * CHUNK)
        hs = slice(h * HEAD_DIM, (h + 1) * HEAD_DIM)
        bs.append(b_chunks[c][:, hs])
        kks.append(1.0 - f_chunks[c][:, hs])
        qqs.append(_silu(q_ref[rows, hs]))
        vvs.append(i_ref[rows, hs])
    atts = []
    for b, kk, qq in zip(bs, kks, qqs):
        att_rows = []
        for blk in range(4):
            r0 = blk * sub
            ref_b = b[r0:r0 + 1, :]
            qe = qq[r0:r0 + sub, :] * jnp.exp(b[r0:r0 + sub, :] - ref_b)
            ke = kk * jnp.exp(jnp.minimum(ref_b - b, EXP_CLAMP))
            att_rows.append(_dot_nt(qe, ke))
        atts.append(jnp.where(causal, jnp.concatenate(att_rows, axis=0), 0.0))
    incs = [_dot_tn(vv, kk * jnp.exp(b[CHUNK - 1:CHUNK, :] - b)) for b, kk, vv in zip(bs, kks, vvs)]
    o_intra = [_dot(att, vv) for att, vv in zip(atts, vvs)]
    states = []
    cur = [st_ref[h] for h in range(N_HEADS)]
    for i, (c, h) in enumerate(probs):
        states.append(cur[h])
        cur[h] = cur[h] * jnp.exp(bs[i][CHUNK - 1:CHUNK, :]) + incs[i]
    for h in range(N_HEADS):
        st_ref[h] = cur[h]
    o_inter = [_dot_nt(qq * jnp.exp(b), st) for qq, b, st in zip(qqs, bs, states)]
    for i, (c, h) in enumerate(probs):
        rows = slice(c * CHUNK, (c + 1) * CHUNK)
        hs = slice(h * HEAD_DIM, (h + 1) * HEAD_DIM)
        y = _rms(o_inter[i] + o_intra[i], ng) * _silu(g_ref[rows, hs])
        y_ref[rows, hs] = y.astype(y_ref.dtype)


def _hgrn2(proj, lb, norm_g, tri, batch, seq):
    t = 256
    spb = seq // t
    col = lambda k: (lambda b, s: (b * spb + s, k))
    const = lambda b, s: (0, 0)
    return pl.pallas_call(
        functools.partial(_hgrn2_kernel, n_chunks=t // CHUNK),
        grid=(batch, spb),
        in_specs=[
            pl.BlockSpec((t, 4 * BRANCH_W), col(0)),
            pl.BlockSpec((1, BRANCH_W), const),
            pl.BlockSpec((1, HEAD_DIM), const),
            pl.BlockSpec((CHUNK, CHUNK), const),
        ],
        out_specs=pl.BlockSpec((t, BRANCH_W), lambda b, s: (b * spb + s, 0)),
        out_shape=jax.ShapeDtypeStruct((batch * seq, BRANCH_W), BF16),
        scratch_shapes=[pltpu.VMEM((N_HEADS, HEAD_DIM, HEAD_DIM), F32)],
        compiler_params=_cparams(2),
        name="hgrn2",
    )(proj, lb, norm_g, tri)


def _softplus(x):
    return jnp.maximum(x, 0.0) + jnp.log(1.0 + jnp.exp(-jnp.abs(x)))


def _gdn_kernel(p_ref, ba_ref, cw_ref, alog_ref, dtb_ref, ng_ref, tri_ref,
                y_ref, xb_ref, qn_ref, kn_ref, vn_ref, st_ref, *, t, n_chunks):
    first = pl.program_id(1) == 0
    q_ref, k_ref, v_ref, g_ref = (p_ref.at[:, k * BRANCH_W:(k + 1) * BRANCH_W] for k in range(4))

    @pl.when(first)
    def _():
        st_ref[...] = jnp.zeros_like(st_ref)
        xb_ref[:, 0:SUBLANES, :] = jnp.zeros((3, SUBLANES, BRANCH_W), F32)

    @pl.when(jnp.logical_not(first))
    def _():
        xb_ref[:, 0:SUBLANES, :] = xb_ref[:, t:t + SUBLANES, :]

    for j, (src, dst, scale) in enumerate(((q_ref, qn_ref, HEAD_DIM ** -0.5), (k_ref, kn_ref, 1.0),
                                           (v_ref, vn_ref, None))):
        xb_ref[j, SUBLANES:SUBLANES + t, :] = src[...]
        acc = jnp.zeros((t, BRANCH_W), F32)
        for tap in range(CONV_K):
            off = SUBLANES - (CONV_K - 1) + tap
            w = cw_ref[tap:tap + 1, j * BRANCH_W:(j + 1) * BRANCH_W]
            acc = acc + xb_ref[j, off:off + t, :] * w
        acc = _silu(acc)
        if scale is None:
            dst[...] = acc
        else:
            for h in range(N_HEADS):
                hs = slice(h * HEAD_DIM, (h + 1) * HEAD_DIM)
                xh = acc[:, hs]
                dst[:, hs] = xh * (lax.rsqrt(jnp.sum(xh * xh, axis=-1, keepdims=True) + EPS) * scale)

    tri = tri_ref[...]
    ng = ng_ref[...]
    row = lax.broadcasted_iota(jnp.int32, (CHUNK, CHUNK), 0)
    col = lax.broadcasted_iota(jnp.int32, (CHUNK, CHUNK), 1)
    causal = col <= row
    strict = col < row
    eye = (col == row).astype(F32)
    blk16 = (row // 16) == (col // 16)
    blk32 = (row // 32) == (col // 32)

    probs = [(c, h) for c in range(n_chunks) for h in range(N_HEADS)]
    qs, ks, vs, betas, bcols, decs = [], [], [], [], [], []
    for c in range(n_chunks):
        rows = slice(c * CHUNK, (c + 1) * CHUNK)
        ba = ba_ref[rows, :]
        beta_all = _sigmoid(ba)
        logdec = -jnp.exp(alog_ref[...]) * _softplus(ba + dtb_ref[...])
        b_all = _dot_split3(tri, logdec)
        b_pad = jnp.concatenate([b_all, jnp.zeros((LANES - CHUNK, LANES), F32)], axis=0)
        b_t = b_pad.T
        for h in range(N_HEADS):
            hs = slice(h * HEAD_DIM, (h + 1) * HEAD_DIM)
            lane = N_HEADS + h
            bcol = b_all[:, lane:lane + 1]
            brow = b_t[lane:lane + 1, 0:CHUNK]
            bcols.append(bcol)
            betas.append(beta_all[:, h:h + 1])
            decs.append(jnp.exp(jnp.where(causal, bcol - brow, NEG)))
            qs.append(qn_ref[rows, hs])
            ks.append(kn_ref[rows, hs])
            vs.append(vn_ref[rows, hs])
    kbs = [k * beta for k, beta in zip(ks, betas)]
    a_mats = [jnp.where(strict, _dot_nt(kb, k) * dec, 0.0) for kb, k, dec in zip(kbs, ks, decs)]
    aqks = [_dot_nt(q, k) * dec for q, k, dec in zip(qs, ks, decs)]
    a_diags = [jnp.where(blk16, a, 0.0) for a in a_mats]
    ps = [eye - a for a in a_diags]
    xs = [_mm(a, a) for a in a_diags]
    for step in range(3):
        pxs = [_mm(p, x) for p, x in zip(ps, xs)]
        if step < 2:
            xs = [_mm(x, x) for x in xs]
        ps = [p + px for p, px in zip(ps, pxs)]
    for low in ([jnp.where(blk32, a, 0.0) - d for a, d in zip(a_mats, a_diags)],
                [jnp.where(blk32, 0.0, a) for a in a_mats]):
        ms = [_mm(p, lo) for p, lo in zip(ps, low)]
        ps = [p - _mm(m, p) for p, m in zip(ps, ms)]
    rhss = [jnp.concatenate([v * beta, kb * jnp.exp(bcol)], axis=1)
            for v, beta, kb, bcol in zip(vs, betas, kbs, bcols)]
    uws = [rhs + _mm(jnp.where(strict, p, 0.0), rhs) for p, rhs in zip(ps, rhss)]
    wqs = [jnp.concatenate([uw[:, HEAD_DIM:], q * jnp.exp(bcol)], axis=0)
           for uw, q, bcol in zip(uws, qs, bcols)]
    kds = [k * jnp.exp(bcol[CHUNK - 1:CHUNK, :] - bcol) for k, bcol in zip(ks, bcols)]
    cur = [st_ref[h] for h in range(N_HEADS)]
    outs = [None] * len(probs)
    for c in range(n_chunks):
        idx = [c * N_HEADS + h for h in range(N_HEADS)]
        wss = [_dot(wqs[i], cur[h]) for h, i in enumerate(idx)]
        v_news = [uws[i][:, :HEAD_DIM] - ws[:CHUNK] for i, ws in zip(idx, wss)]
        o_new = [_dot(aqks[i], v_new) for i, v_new in zip(idx, v_news)]
        incs = [_dot_tn(kds[i], v_new) for i, v_new in zip(idx, v_news)]
        for h, i in enumerate(idx):
            outs[i] = wss[h][CHUNK:] + o_new[h]
            cur[h] = cur[h] * jnp.exp(bcols[i][CHUNK - 1:CHUNK, :]) + incs[h]
    for h in range(N_HEADS):
        st_ref[h] = cur[h]
    for i, (c, h) in enumerate(probs):
        rows = slice(c * CHUNK, (c + 1) * CHUNK)
        hs = slice(h * HEAD_DIM, (h + 1) * HEAD_DIM)
        y = _rms(outs[i], ng) * _silu(g_ref[rows, hs])
        y_ref[rows, hs] = y.astype(y_ref.dtype)


def _gdn(proj, ba, conv_w, alog_row, dtb_row, norm_g, tri, batch, seq):
    t = 256
    spb = seq // t
    col = lambda k: (lambda b, s: (b * spb + s, k))
    const = lambda b, s: (0, 0)
    return pl.pallas_call(
        functools.partial(_gdn_kernel, t=t, n_chunks=t // CHUNK),
        grid=(batch, spb),
        in_specs=[
            pl.BlockSpec((t, 4 * BRANCH_W), col(1)),
            pl.BlockSpec((t, LANES), col(0)),
            pl.BlockSpec((CONV_K, 3 * BRANCH_W), const),
            pl.BlockSpec((1, LANES), const),
            pl.BlockSpec((1, LANES), const),
            pl.BlockSpec((1, HEAD_DIM), const),
            pl.BlockSpec((CHUNK, CHUNK), const),
        ],
        out_specs=pl.BlockSpec((t, BRANCH_W), lambda b, s: (b * spb + s, 0)),
        out_shape=jax.ShapeDtypeStruct((batch * seq, BRANCH_W), BF16),
        scratch_shapes=[
            pltpu.VMEM((3, t + 2 * SUBLANES, BRANCH_W), F32),
            pltpu.VMEM((t, BRANCH_W), F32),
            pltpu.VMEM((t, BRANCH_W), F32),
            pltpu.VMEM((t, BRANCH_W), F32),
            pltpu.VMEM((N_HEADS, HEAD_DIM, HEAD_DIM), F32),
        ],
        compiler_params=_cparams(2),
        name="gdn",
    )(proj, ba, conv_w, alog_row, dtb_row, norm_g, tri)


def _merge_kernel(yh_ref, yg_ref, gate_ref, x_ref, gt_ref, wbh_ref, wbg_ref, wo_ref,
                  g2_ref, sc_ref, sh_ref, wq_ref, xo_ref, h2t_ref, q_ref):
    d = x_ref.shape[1]
    merged = (_sigmoid(gate_ref[:, :d]) * jnp.dot(yh_ref[...], wbh_ref[...], preferred_element_type=F32)
              + _sigmoid(gate_ref[:, d:]) * jnp.dot(yg_ref[...], wbg_ref[...], preferred_element_type=F32))
    y = jnp.dot(merged.astype(BF16), wo_ref[...], preferred_element_type=F32)
    xn = x_ref[...] + gt_ref[0] * y
    xo_ref[...] = xn
    h2 = _rms(xn, g2_ref[...]) * (1.0 + sc_ref[0]) + sh_ref[0]
    h2t_ref[...] = pltpu.bitcast(h2.T.astype(BF16), jnp.uint32)
    q_ref[...] = jnp.dot(h2.astype(BF16), wq_ref[...], preferred_element_type=F32).astype(q_ref.dtype)


def _merge(y_hg, y_gdn, proj, x, gt1, w_bh, w_bg, w_out, g2, sc2, sh2, wq, seq):
    n, d = x.shape
    tm = 512
    tpb = seq // tm
    qw = wq.shape[1]
    const = lambda i: (0, 0)
    per_b = lambda i: (i // tpb, 0, 0)
    gate_blk = 8 * BRANCH_W // (2 * D_MODEL)
    return pl.pallas_call(
        _merge_kernel,
        grid=(n // tm,),
        in_specs=[
            pl.BlockSpec((tm, BRANCH_W), lambda i: (i, 0)),
            pl.BlockSpec((tm, BRANCH_W), lambda i: (i, 0)),
            pl.BlockSpec((tm, 2 * d), lambda i: (i, gate_blk)),
            pl.BlockSpec((tm, d), lambda i: (i, 0)),
            pl.BlockSpec((1, 1, d), per_b),
            pl.BlockSpec((BRANCH_W, d), const),
            pl.BlockSpec((BRANCH_W, d), const),
            pl.BlockSpec((d, d), const),
            pl.BlockSpec((1, d), const),
            pl.BlockSpec((1, 1, d), per_b),
            pl.BlockSpec((1, 1, d), per_b),
            pl.BlockSpec((d, qw), const),
        ],
        out_specs=[
            pl.BlockSpec((tm, d), lambda i: (i, 0)),
            pl.BlockSpec((d // 2, tm), lambda i: (0, i)),
            pl.BlockSpec((tm, qw), lambda i: (i, 0)),
        ],
        out_shape=[
            jax.ShapeDtypeStruct((n, d), F32),
            jax.ShapeDtypeStruct((d // 2, n), jnp.uint32),
            jax.ShapeDtypeStruct((n, qw), BF16),
        ],
        compiler_params=_cparams(1),
        name="merge",
    )(y_hg, y_gdn, proj, x, gt1, w_bh, w_bg, w_out, g2, sc2, sh2, wq)


def _top_ranked(s, k):
    rid = lax.broadcasted_iota(jnp.int32, (k, s.shape[1]), 0)
    vals = jnp.zeros((k, s.shape[1]), F32)
    for r in range(k):
        m = jnp.max(s, axis=0, keepdims=True)
        vals = jnp.where(rid == r, m, vals)
        s = jnp.where(s == m, -(RANK_BASE + r * RANK_STEP), s)
    rank = jnp.where(s < -0.5 * RANK_BASE, jnp.round((-s - RANK_BASE) * (1.0 / RANK_STEP)), float(k))
    return vals, rank


def _route_kernel(q_ref, sk_ref, r1_ref, e1_ref, cnt_ref, e0_ref):
    p = q_ref.shape[0]
    kk = PEER_TOPK
    rid = lax.broadcasted_iota(jnp.int32, (SUBLANES, p), 0)
    for h in range(PEER_HEADS):
        q0 = q_ref[:, (2 * h) * PEER_HALF:(2 * h + 1) * PEER_HALF]
        q1 = q_ref[:, (2 * h + 1) * PEER_HALF:(2 * h + 2) * PEER_HALF]
        s0 = _dot_nt(sk_ref[h, 0], q0)
        s1 = _dot_nt(sk_ref[h, 1], q1)
        v0, rank0 = _top_ranked(s0, kk)
        v1, rank1 = _top_ranked(s1, kk)
        lo, hi = v1[0:SUBLANES, :], v1[SUBLANES:kk, :]
        pieces = [v0[0:1, :] + lo, v0[0:1, :] + hi, v0[1:2, :] + lo]
        for r0 in range(2, SUBLANES):
            pieces.append(jnp.where(rid < kk // (r0 + 1), v0[r0:r0 + 1, :] + lo, NEG))
        pieces.append(v0[SUBLANES:kk, :] + v1[0:1, :])
        work = list(pieces)
        tops = []
        for _ in range(kk):
            m = work[0]
            for a in work[1:]:
                m = jnp.maximum(m, a)
            m = jnp.max(m, axis=0, keepdims=True)
            tops.append(m)
            work = [jnp.where(a == m, NEG, a) for a in work]
        tau = tops[kk - 1]
        zsum = jnp.zeros_like(tau)
        for tk in tops:
            zsum = zsum + jnp.exp(tk - tops[0])
        sel = [jnp.where(pc >= tau, 1.0, 0.0) for pc in pieces]
        n_sel = [jnp.sum(sel[0] + sel[1], axis=0, keepdims=True)]
        n_sel += [jnp.sum(sel[r0 + 1], axis=0, keepdims=True) for r0 in range(1, SUBLANES)]
        n_high = jnp.sum(sel[SUBLANES + 1], axis=0, keepdims=True)
        high = (rank0 >= float(SUBLANES)) & (rank0 < float(SUBLANES) + n_high)
        cnt = jnp.where(high, 1.0, 0.0)
        for r0 in range(SUBLANES):
            cnt = jnp.where(rank0 == float(r0), n_sel[r0], cnt)
        r1_ref[h] = pltpu.bitcast(rank1.astype(BF16), jnp.uint32)
        e1_ref[h] = pltpu.bitcast(jnp.exp(s1 - v1[0:1, :]).astype(BF16), jnp.uint32)
        cnt_ref[h] = cnt
        e0_ref[h] = jnp.exp(s0 - v0[0:1, :]) * (0.5 / zsum)


def _route(q, subkeys_bf16):
    n, qw = q.shape
    p = LANES
    shp = jax.ShapeDtypeStruct((PEER_HEADS, PEER_NKEYS, n), F32)
    spec = pl.BlockSpec((PEER_HEADS, PEER_NKEYS, p), lambda i: (0, 0, i))
    shp_packed = jax.ShapeDtypeStruct((PEER_HEADS, PEER_NKEYS // 2, n), jnp.uint32)
    spec_packed = pl.BlockSpec((PEER_HEADS, PEER_NKEYS // 2, p), lambda i: (0, 0, i))
    return pl.pallas_call(
        _route_kernel,
        grid=(n // p,),
        in_specs=[
            pl.BlockSpec((p, qw), lambda i: (i, 0)),
            pl.BlockSpec((PEER_HEADS, 2, PEER_NKEYS, PEER_HALF), lambda i: (0, 0, 0, 0)),
        ],
        out_specs=[spec_packed, spec_packed, spec, spec],
        out_shape=[shp_packed, shp_packed, shp, shp],
        compiler_params=_cparams(1),
        name="peer_route",
    )(q, subkeys_bf16)


def _experts_kernel(ht_ref, u_ref, vt_ref, r1_ref, e1_ref, cnt_ref, e0_ref, x_ref, gt_ref, fg_ref, o_ref,
                    acc_ref, zt_ref, act_ref, *, ib, parts, final):
    step = pl.program_id(1)

    @pl.when(step == 0)
    def _():
        acc_ref[...] = jnp.zeros_like(acc_ref)

    hbt = pltpu.bitcast(ht_ref[...], BF16)
    p = hbt.shape[1]
    per_part = ib // parts
    width = per_part * PEER_NKEYS
    for part in range(parts):
        rows = slice(part * width, (part + 1) * width)
        u_part = pltpu.bitcast(u_ref[part * width // 2:(part + 1) * width // 2, :], BF16)
        zt_ref[rows, :] = jnp.dot(u_part, hbt, preferred_element_type=F32)
    def gate_pair(first):
        iis = (first, first + 1)
        bcast = lambda ref, h, ii: jnp.broadcast_to(ref[h, ii:ii + 1, :], (BF16_ROWS, p)).astype(BF16)
        cnts = [[bcast(cnt_ref, h, ii) for h in range(PEER_HEADS)] for ii in iis]
        e0s = [[bcast(e0_ref, h, ii) for h in range(PEER_HEADS)] for ii in iis]
        for grp in range(PEER_NKEYS // BF16_ROWS):
            packed = slice(grp * BF16_ROWS // 2, (grp + 1) * BF16_ROWS // 2)
            gs = [jnp.zeros((BF16_ROWS, p), BF16) for _ in iis]
            for h in range(PEER_HEADS):
                r1 = pltpu.bitcast(r1_ref[h, packed, :], BF16)
                e1 = pltpu.bitcast(e1_ref[h, packed, :], BF16)
                for k in range(len(iis)):
                    gs[k] = gs[k] + e0s[k][h] * jnp.where(r1 < cnts[k][h], e1, jnp.zeros((), BF16))
            for k, ii in enumerate(iis):
                rows = slice(ii * PEER_NKEYS + grp * BF16_ROWS, ii * PEER_NKEYS + (grp + 1) * BF16_ROWS)
                z = zt_ref[rows, :]
                zg = (z * (1.0 + lax.erf(z * (2.0 ** -0.5)))).astype(BF16)
                act_ref[rows, :] = zg * gs[k]

    for part in range(parts):
        for first in range(part * per_part, (part + 1) * per_part, 2):
            gate_pair(first)
        rows = slice(part * width, (part + 1) * width)
        acc_ref[...] += jnp.dot(pltpu.bitcast(vt_ref[:, rows], BF16), act_ref[rows, :],
                                preferred_element_type=F32)

    @pl.when(step == pl.num_programs(1) - 1)
    def _():
        xn = x_ref[...] + gt_ref[0] * acc_ref[...].T
        o_ref[...] = _rms(xn, fg_ref[...]) if final else xn


def _experts(h2t, u_packed, vt_packed, layer, r1, e1, cnt, e0, x, gt2, final_g, final, seq):
    n, d = x.shape
    p = min(1024, seq)
    ib = 8
    tpb = seq // p
    blk = ib * PEER_NKEYS
    stat = pl.BlockSpec((PEER_HEADS, PEER_NKEYS // 2, p), lambda i, j: (0, 0, i))
    per_i = pl.BlockSpec((PEER_HEADS, ib, p), lambda i, j: (0, j, i))
    return pl.pallas_call(
        functools.partial(_experts_kernel, ib=ib, parts=2, final=final),
        grid=(n // p, PEER_NKEYS // ib),
        in_specs=[
            pl.BlockSpec((d // 2, p), lambda i, j: (0, i)),
            pl.BlockSpec((None, blk // 2, d), lambda i, j: (layer, j, 0)),
            pl.BlockSpec((None, d // 2, blk), lambda i, j: (layer, 0, j)),
            stat, stat, per_i, per_i,
            pl.BlockSpec((p, d), lambda i, j: (i, 0)),
            pl.BlockSpec((1, 1, d), lambda i, j: (i // tpb, 0, 0)),
            pl.BlockSpec((1, d), lambda i, j: (0, 0)),
        ],
        out_specs=pl.BlockSpec((p, d), lambda i, j: (i, 0)),
        out_shape=jax.ShapeDtypeStruct((n, d), F32),
        scratch_shapes=[pltpu.VMEM((d, p), F32), pltpu.VMEM((blk, p), F32), pltpu.VMEM((blk, p), BF16)],
        compiler_params=_cparams(2),
        name="peer_experts",
    )(h2t, u_packed, vt_packed, r1, e1, cnt, e0, x, gt2, final_g)


def _pack_tables_kernel(u_ref, v_ref, uo_ref, vo_ref):
    uo_ref[...] = pltpu.bitcast(u_ref[...].astype(BF16), jnp.uint32)
    vo_ref[...] = pltpu.bitcast(v_ref[...].T.astype(BF16), jnp.uint32)


def _pack_tables(peer_u, peer_v):
    depth, experts, d = peer_u.shape
    te = 1024
    return pl.pallas_call(
        _pack_tables_kernel,
        grid=(depth, experts // te),
        in_specs=[
            pl.BlockSpec((None, te, d), lambda l, j: (l, j, 0)),
            pl.BlockSpec((None, te, d), lambda l, j: (l, j, 0)),
        ],
        out_specs=[
            pl.BlockSpec((None, te // 2, d), lambda l, j: (l, j, 0)),
            pl.BlockSpec((None, d // 2, te), lambda l, j: (l, 0, j)),
        ],
        out_shape=[
            jax.ShapeDtypeStruct((depth, experts // 2, d), jnp.uint32),
            jax.ShapeDtypeStruct((depth, d // 2, experts), jnp.uint32),
        ],
        compiler_params=_cparams(2),
        name="pack_tables",
    )(peer_u, peer_v)


def _lane_row(vals_lo, vals_hi):
    row = jnp.zeros((1, LANES), F32)
    row = row.at[0, 0:N_HEADS].set(vals_lo)
    return row.at[0, N_HEADS:2 * N_HEADS].set(vals_hi)


def kernel(x, c, ada_w, ada_b, norm1_g, norm2_g, final_g, w_in, hg_lb_logits, hg_norm_g, gdn_conv_w,
           gdn_a_log, gdn_dt_bias, gdn_norm_g, w_branch_hg, w_branch_gdn, w_out, peer_wq, peer_subkeys,
           peer_u, peer_v):
    batch, seq, d = x.shape
    depth = ada_w.shape[0]
    n = batch * seq

    sm = jax.nn.softmax(hg_lb_logits.astype(F32), axis=0)
    lower_bounds = jnp.cumsum(sm, axis=0) - sm[0:1]

    c_pad = jnp.zeros((SUBLANES, d), F32).at[:batch].set(c)
    mod = _adaln(c_pad, ada_w, ada_b)[:, :batch, :]

    u_packed, vt_packed = _pack_tables(peer_u, peer_v)
    tri = jnp.tril(jnp.ones((CHUNK, CHUNK), F32)).astype(BF16)
    zeros4 = jnp.zeros((N_HEADS,), F32)

    xf = x.reshape(n, d)
    for l in range(depth):
        sh1, sc1, gt1, sh2, sc2, gt2 = [m.reshape(batch, 1, d) for m in jnp.split(mod[l], 6, axis=-1)]
        w_l = w_in[l]
        k0 = 8 * BRANCH_W
        w_main = jnp.concatenate([w_l[:, :k0], w_l[:, k0 + 2 * N_HEADS:]], axis=1).astype(BF16)
        w_small = jnp.zeros((d, LANES), F32).at[:, :2 * N_HEADS].set(w_l[:, k0:k0 + 2 * N_HEADS]).astype(BF16)
        proj, ba = _proj(xf, norm1_g[l].reshape(1, d), sc1, sh1, w_main, w_small, seq)

        y_hg = _hgrn2(proj, lower_bounds[l].reshape(1, BRANCH_W), hg_norm_g[l].reshape(1, HEAD_DIM), tri,
                      batch, seq)
        y_gdn = _gdn(proj, ba, gdn_conv_w[l].astype(F32), _lane_row(zeros4, gdn_a_log[l].astype(F32)),
                     _lane_row(zeros4, gdn_dt_bias[l].astype(F32)), gdn_norm_g[l].reshape(1, HEAD_DIM), tri,
                     batch, seq)

        xf, h2t, q = _merge(y_hg, y_gdn, proj, xf, gt1, w_branch_hg[l].astype(BF16),
                            w_branch_gdn[l].astype(BF16), w_out[l].astype(BF16),
                            norm2_g[l].reshape(1, d), sc2, sh2, peer_wq[l].astype(BF16), seq)

        r1, e1, cnt, e0 = _route(q, peer_subkeys[l].astype(BF16))
        xf = _experts(h2t, u_packed, vt_packed, l, r1, e1, cnt, e0, xf, gt2, final_g.reshape(1, d),
                      l == depth - 1, seq)

    return xf.reshape(batch, seq, d)
```
